```python
import math
import jax, jax.numpy as jnp
from jax import lax
import numpy as np

D_MODEL = 2048
BATCH = 1
SEQ = 8192
DEPTH = 4

N_MIXERS = 2
CONV_WIDTH = 31
SSM_GROUP = 16
SSM_GROUPS = D_MODEL // SSM_GROUP
SSM_STATE = 64
D_FF = 4 * D_MODEL
N_CONV_LAYERS = (DEPTH + 1) // 2
N_SSM_LAYERS = DEPTH // 2
EPS = 1e-6
DT_MIN = 1e-3
DT_MAX = 1e-1

kernel_name = "interleaved_conformer_conv_s5_hybrid"


def rms_norm(x, g):
    xf = x.astype(jnp.float32)
    y = xf * lax.rsqrt(jnp.mean(xf * xf, axis=-1, keepdims=True) + EPS)
    return (y * g.astype(jnp.float32)).astype(x.dtype)


def layer_norm(x, g, b):
    xf = x.astype(jnp.float32)
    mu = jnp.mean(xf, axis=-1, keepdims=True)
    xc = xf - mu
    var = jnp.mean(xc * xc, axis=-1, keepdims=True)
    y = xc * lax.rsqrt(var + EPS) * g.astype(jnp.float32) + b.astype(jnp.float32)
    return y.astype(x.dtype)


def conv_module(h, w_in, b_in, dw, dw_b, ln_g, ln_b, w_out, b_out):
    u = h @ w_in + b_in
    a, gate = jnp.split(u, 2, axis=-1)
    v = a * jax.nn.sigmoid(gate)
    v = lax.conv_general_dilated(
        v, dw[:, None, :], window_strides=(1,),
        padding=[(CONV_WIDTH - 1, 0)],
        dimension_numbers=("NWC", "WIO", "NWC"),
        feature_group_count=D_MODEL) + dw_b
    v = layer_norm(v, ln_g, ln_b)
    v = jax.nn.silu(v)
    return v @ w_out + b_out


def ssm_module(h, lam_re, lam_im, log_dt, b_re, b_im, c_re, c_im, d_skip, w_glu):
    f32 = jnp.float32
    bsz, L, _ = h.shape
    u = h.astype(f32).reshape(bsz, L, SSM_GROUPS, SSM_GROUP)
    dt = jnp.exp(log_dt.astype(f32))[:, None]
    lr = lam_re.astype(f32)
    li = lam_im.astype(f32)
    mag = jnp.exp(lr * dt)
    ab_re = mag * jnp.cos(li * dt)
    ab_im = mag * jnp.sin(li * dt)
    nr = ab_re - 1.0
    ni = ab_im
    den = lr * lr + li * li
    k_re = ((nr * lr + ni * li) / den)[..., None]
    k_im = ((ni * lr - nr * li) / den)[..., None]
    br = b_re.astype(f32)
    bi = b_im.astype(f32)
    bb_re = k_re * br - k_im * bi
    bb_im = k_re * bi + k_im * br
    bu_re = jnp.einsum("blgc,gpc->blgp", u, bb_re)
    bu_im = jnp.einsum("blgc,gpc->blgp", u, bb_im)
    a_re = jnp.broadcast_to(ab_re, bu_re.shape)
    a_im = jnp.broadcast_to(ab_im, bu_im.shape)

    def combine(e1, e2):
        a1r, a1i, b1r, b1i = e1
        a2r, a2i, b2r, b2i = e2
        return (a2r * a1r - a2i * a1i,
                a2r * a1i + a2i * a1r,
                a2r * b1r - a2i * b1i + b2r,
                a2r * b1i + a2i * b1r + b2i)

    _, _, s_re, s_im = lax.associative_scan(combine, (a_re, a_im, bu_re, bu_im), axis=1)
    y = (jnp.einsum("blgp,gcp->blgc", s_re, c_re.astype(f32))
         - jnp.einsum("blgp,gcp->blgc", s_im, c_im.astype(f32)))
    y = y.reshape(bsz, L, D_MODEL) + d_skip.astype(f32) * h.astype(f32)
    y = jax.nn.gelu(y).astype(h.dtype)
    z = y @ w_glu
    val, gate = jnp.split(z, 2, axis=-1)
    return val * jax.nn.sigmoid(gate)


def mlp(h, w_up, w_down):
    a = jax.nn.relu(h @ w_up)
    return (a * a) @ w_down


def setup_inputs(seed: int = 0) -> dict:
    key = jax.random.key(seed)
    ks = jax.random.split(key, 24)
    f32 = jnp.float32
    D, NC, NS, G, P, C = D_MODEL, N_CONV_LAYERS, N_SSM_LAYERS, SSM_GROUPS, SSM_STATE, SSM_GROUP

    def nrm(k, shape, scale):
        return jax.random.normal(k, shape, f32) * scale

    x = jax.random.normal(ks[0], (BATCH, SEQ, D), f32)
    mix_norm = 1.0 + nrm(ks[1], (DEPTH, D), 0.02)
    conv_w_in = nrm(ks[2], (NC, D, 2 * D), D ** -0.5)
    conv_b_in = nrm(ks[3], (NC, 2 * D), 0.02)
    conv_dw = nrm(ks[4], (NC, CONV_WIDTH, D), CONV_WIDTH ** -0.5)
    conv_dw_b = nrm(ks[5], (NC, D), 0.02)
    conv_ln_g = 1.0 + nrm(ks[6], (NC, D), 0.02)
    conv_ln_b = nrm(ks[7], (NC, D), 0.02)
    conv_w_out = nrm(ks[8], (NC, D, D), D ** -0.5)
    conv_b_out = nrm(ks[9], (NC, D), 0.02)
    ssm_lambda_re = -0.5 + nrm(ks[10], (NS, G, P), 0.01)
    ssm_lambda_im = (math.pi * jnp.arange(P, dtype=f32))[None, None, :] + nrm(ks[11], (NS, G, P), 0.01)
    ssm_log_dt = jax.random.uniform(ks[12], (NS, G), f32, math.log(DT_MIN), math.log(DT_MAX))
    ssm_b_re = nrm(ks[13], (NS, G, P, C), (2 * C) ** -0.5)
    ssm_b_im = nrm(ks[14], (NS, G, P, C), (2 * C) ** -0.5)
    ssm_c_re = nrm(ks[15], (NS, G, C, P), P ** -0.5)
    ssm_c_im = nrm(ks[16], (NS, G, C, P), P ** -0.5)
    ssm_d = 1.0 + nrm(ks[17], (NS, D), 0.1)
    ssm_w_glu = nrm(ks[18], (NS, D, 2 * D), D ** -0.5)
    mlp_norm = 1.0 + nrm(ks[19], (DEPTH, D), 0.02)
    mlp_w_up = nrm(ks[20], (DEPTH, D, D_FF), D ** -0.5)
    mlp_w_down = nrm(ks[21], (DEPTH, D_FF, D), D_FF ** -0.5)
    final_norm = 1.0 + nrm(ks[22], (D,), 0.02)
    return {"x": x, "mix_norm": mix_norm,
            "conv_w_in": conv_w_in, "conv_b_in": conv_b_in, "conv_dw": conv_dw,
            "conv_dw_b": conv_dw_b, "conv_ln_g": conv_ln_g, "conv_ln_b": conv_ln_b,
            "conv_w_out": conv_w_out, "conv_b_out": conv_b_out,
            "ssm_lambda_re": ssm_lambda_re, "ssm_lambda_im": ssm_lambda_im,
            "ssm_log_dt": ssm_log_dt, "ssm_b_re": ssm_b_re, "ssm_b_im": ssm_b_im,
            "ssm_c_re": ssm_c_re, "ssm_c_im": ssm_c_im, "ssm_d": ssm_d,
            "ssm_w_glu": ssm_w_glu, "mlp_norm": mlp_norm, "mlp_w_up": mlp_w_up,
            "mlp_w_down": mlp_w_down, "final_norm": final_norm}


def reference(x, mix_norm, conv_w_in, conv_b_in, conv_dw, conv_dw_b, conv_ln_g, conv_ln_b,
              conv_w_out, conv_b_out, ssm_lambda_re, ssm_lambda_im, ssm_log_dt, ssm_b_re,
              ssm_b_im, ssm_c_re, ssm_c_im, ssm_d, ssm_w_glu, mlp_norm, mlp_w_up,
              mlp_w_down, final_norm):
    for i in range(DEPTH):
        h = rms_norm(x, mix_norm[i])
        j = i // N_MIXERS
        if i % N_MIXERS == 0:
            x = x + conv_module(h, conv_w_in[j], conv_b_in[j], conv_dw[j], conv_dw_b[j],
                                conv_ln_g[j], conv_ln_b[j], conv_w_out[j], conv_b_out[j])
        else:
            x = x + ssm_module(h, ssm_lambda_re[j], ssm_lambda_im[j], ssm_log_dt[j],
                               ssm_b_re[j], ssm_b_im[j], ssm_c_re[j], ssm_c_im[j],
                               ssm_d[j], ssm_w_glu[j])
        h = rms_norm(x, mlp_norm[i])
        x = x + mlp(h, mlp_w_up[i], mlp_w_down[i])
    return rms_norm(x, final_norm)
```

```python
import functools

import jax
import jax.numpy as jnp
from jax import lax
from jax.experimental import pallas as pl
from jax.experimental.pallas import tpu as pltpu

D_MODEL = 2048
SEQ = 8192
DEPTH = 4
CONV_WIDTH = 31
SSM_GROUP = 16
SSM_GROUPS = D_MODEL // SSM_GROUP
SSM_STATE = 64
D_FF = 4 * D_MODEL
EPS = 1e-6

SUBLANES = 8
LANES = 128
VMEM_LIMIT_BYTES = 56 * 1024 * 1024

TM = 1024
TN = 512
TM_MLP = 512
TF_MLP = 1024

TIME_BLOCK = 512
TIME_SEG = TIME_BLOCK // SUBLANES

CONV_HALO = 32 * SUBLANES
CONV_ROWS = 32
CONV_COLS = 512
LN_ROWS = 16

SSM_TILE_GROUPS = 16
SSM_TILE_CH = SSM_TILE_GROUPS * SSM_GROUP
SSM_TILE_STATES = SSM_TILE_GROUPS * SSM_STATE
SSM_TILES = SSM_GROUPS // SSM_TILE_GROUPS


def _params(*sem):
    return pltpu.CompilerParams(dimension_semantics=sem, vmem_limit_bytes=VMEM_LIMIT_BYTES)


def _rms(x, g):
    ms = jnp.mean(x * x, axis=-1, keepdims=True)
    return x * lax.rsqrt(ms + EPS) * g


def _conv_in_kernel(x_ref, g_ref, wa_ref, wg_ref, ba_ref, bg_ref, o_ref, h_ref):
    @pl.when(pl.program_id(1) == 0)
    def _():
        h_ref[...] = _rms(x_ref[...], g_ref[...]).astype(jnp.bfloat16)

    h = h_ref[...]
    a = jnp.dot(h, wa_ref[...], preferred_element_type=jnp.float32) + ba_ref[...]
    gate = jnp.dot(h, wg_ref[...], preferred_element_type=jnp.float32) + bg_ref[...]
    o_ref[...] = a * jax.nn.sigmoid(gate)


def _conv_in(x, g, w_in, b_in):
    n_col = D_MODEL // TN
    return pl.pallas_call(
        _conv_in_kernel,
        grid=(SEQ // TM, n_col),
        in_specs=[
            pl.BlockSpec((TM, D_MODEL), lambda i, n: (i, 0)),
            pl.BlockSpec((1, D_MODEL), lambda i, n: (0, 0)),
            pl.BlockSpec((D_MODEL, TN), lambda i, n: (0, n)),
            pl.BlockSpec((D_MODEL, TN), lambda i, n: (0, n + n_col)),
            pl.BlockSpec((1, TN), lambda i, n: (0, n)),
            pl.BlockSpec((1, TN), lambda i, n: (0, n + n_col)),
        ],
        out_specs=pl.BlockSpec((TM, TN), lambda i, n: (i, n)),
        out_shape=jax.ShapeDtypeStruct((SEQ, D_MODEL), jnp.float32),
        scratch_shapes=[pltpu.VMEM((TM, D_MODEL), jnp.bfloat16)],
        compiler_params=_params("parallel", "arbitrary"),
        name="conv_in",
    )(x, g, w_in, w_in, b_in, b_in)


def _dwconv_kernel(v_ref, dw_ref, dwb_ref, lg_ref, lb_ref, o_ref, ext_ref, tail_ref, cv_ref):
    @pl.when(pl.program_id(0) == 0)
    def _():
        tail_ref[...] = jnp.zeros(tail_ref.shape, jnp.float32)

    tail_start = TIME_BLOCK - CONV_HALO
    first_segment = lax.broadcasted_iota(jnp.int32, (SUBLANES, D_MODEL), 0) == 0

    def halo_rows(m, carry):
        r0 = pl.multiple_of(m * SUBLANES, SUBLANES)
        cur = pltpu.roll(v_ref[pl.ds(tail_start + r0, SUBLANES), :], 1, 0)
        prev = pltpu.roll(tail_ref[pl.ds(r0, SUBLANES), :], 1, 0)
        ext_ref[pl.ds(r0, SUBLANES), :] = jnp.where(first_segment, prev, cur)
        return carry

    lax.fori_loop(0, CONV_HALO // SUBLANES, halo_rows, 0)
    tail_ref[...] = v_ref[tail_start:, :]
    ext_ref[CONV_HALO:, :] = v_ref[...]
    first_tap = CONV_HALO - (CONV_WIDTH - 1) * SUBLANES

    def conv_rows(r, carry):
        r0 = pl.multiple_of(r * CONV_ROWS, CONV_ROWS)
        for c in range(D_MODEL // CONV_COLS):
            cols = slice(c * CONV_COLS, (c + 1) * CONV_COLS)
            acc = jnp.zeros((CONV_ROWS, CONV_COLS), jnp.float32)
            for k in range(CONV_WIDTH):
                taps = ext_ref[pl.ds(r0 + first_tap + k * SUBLANES, CONV_ROWS), cols]
                acc = acc + taps * dw_ref[k:k + 1, cols]
            cv_ref[pl.ds(r0, CONV_ROWS), cols] = acc + dwb_ref[:, cols]
        return carry

    lax.fori_loop(0, TIME_BLOCK // CONV_ROWS, conv_rows, 0)

    def norm_rows(r, carry):
        r0 = pl.multiple_of(r * LN_ROWS, LN_ROWS)
        c = cv_ref[pl.ds(r0, LN_ROWS), :]
        mu = jnp.mean(c, axis=-1, keepdims=True)
        xc = c - mu
        var = jnp.mean(xc * xc, axis=-1, keepdims=True)
        y = xc * lax.rsqrt(var + EPS) * lg_ref[...] + lb_ref[...]
        o_ref[pl.ds(r0, LN_ROWS), :] = (y * jax.nn.sigmoid(y)).astype(jnp.bfloat16)
        return carry

    lax.fori_loop(0, TIME_BLOCK // LN_ROWS, norm_rows, 0)


def _dwconv(v, dw, dw_b, ln_g, ln_b):
    row = lambda i: (0, 0)
    return pl.pallas_call(
        _dwconv_kernel,
        grid=(SEQ // TIME_BLOCK,),
        in_specs=[
            pl.BlockSpec((TIME_BLOCK, D_MODEL), lambda i: (i, 0)),
            pl.BlockSpec((CONV_WIDTH, D_MODEL), row),
            pl.BlockSpec((1, D_MODEL), row),
            pl.BlockSpec((1, D_MODEL), row),
            pl.BlockSpec((1, D_MODEL), row),
        ],
        out_specs=pl.BlockSpec((TIME_BLOCK, D_MODEL), lambda i: (i, 0)),
        out_shape=jax.ShapeDtypeStruct((SEQ, D_MODEL), jnp.bfloat16),
        scratch_shapes=[
            pltpu.VMEM((CONV_HALO + TIME_BLOCK, D_MODEL), jnp.float32),
            pltpu.VMEM((CONV_HALO, D_MODEL), jnp.float32),
            pltpu.VMEM((TIME_BLOCK, D_MODEL), jnp.float32),
        ],
        compiler_params=_params("arbitrary"),
        name="dwconv_ln_silu",
    )(v, dw, dw_b, ln_g, ln_b)


def _conv_out_kernel(v_ref, w_ref, b_ref, x_ref, o_ref):
    y = jnp.dot(v_ref[...], w_ref[...], preferred_element_type=jnp.float32)
    o_ref[...] = x_ref[...] + (y + b_ref[...])


def _conv_out(v, w_out, b_out, x):
    return pl.pallas_call(
        _conv_out_kernel,
        grid=(SEQ // TM, D_MODEL // TN),
        in_specs=[
            pl.BlockSpec((TM, D_MODEL), lambda i, n: (i, 0)),
            pl.BlockSpec((D_MODEL, TN), lambda i, n: (0, n)),
            pl.BlockSpec((1, TN), lambda i, n: (0, n)),
            pl.BlockSpec((TM, TN), lambda i, n: (i, n)),
        ],
        out_specs=pl.BlockSpec((TM, TN), lambda i, n: (i, n)),
        out_shape=jax.ShapeDtypeStruct((SEQ, D_MODEL), jnp.float32),
        compiler_params=_params("parallel", "parallel"),
        name="conv_out",
    )(v, w_out, b_out, x)


def _rms_kernel(x_ref, g_ref, o_ref):
    o_ref[...] = _rms(x_ref[...], g_ref[...])


def _rms_norm(x, g):
    return pl.pallas_call(
        _rms_kernel,
        grid=(SEQ // TM,),
        in_specs=[
            pl.BlockSpec((TM, D_MODEL), lambda i: (i, 0)),
            pl.BlockSpec((1, D_MODEL), lambda i: (0, 0)),
        ],
        out_specs=pl.BlockSpec((TM, D_MODEL), lambda i: (i, 0)),
        out_shape=jax.ShapeDtypeStruct((SEQ, D_MODEL), jnp.float32),
        compiler_params=_params("parallel"),
        name="rms_norm",
    )(x, g)


def _ssm_kernel(h_ref, bw_ref, cw_ref, ap_ref, d_ref, o_ref, bu_ref, s_ref, carry_ref):
    ns = SSM_TILE_STATES

    @pl.when(pl.program_id(1) == 0)
    def _():
        carry_ref[...] = jnp.zeros(carry_ref.shape, jnp.float32)

    bu_ref[...] = jnp.dot(h_ref[...].astype(jnp.bfloat16), bw_ref[0], preferred_element_type=jnp.float32)

    a_re = ap_ref[0, 0]
    a_im = ap_ref[0, 1]

    def advance(i, s_re, s_im):
        r0 = pl.multiple_of(i * SUBLANES, SUBLANES)
        b_re = bu_ref[pl.ds(r0, SUBLANES), 0:ns]
        b_im = bu_ref[pl.ds(r0, SUBLANES), ns:2 * ns]
        return (a_re * s_re - a_im * s_im + b_re, a_re * s_im + a_im * s_re + b_im)

    in_re = carry_ref[0]
    in_im = carry_ref[1]
    f_re, f_im = lax.fori_loop(0, TIME_SEG, lambda i, s: advance(i, *s), (in_re, in_im))

    for level, shift in enumerate((1, 2, 4)):
        p_re = ap_ref[0, 2 + 2 * level]
        p_im = ap_ref[0, 3 + 2 * level]
        r_re = pltpu.roll(f_re, shift, 0)
        r_im = pltpu.roll(f_im, shift, 0)
        f_re, f_im = (f_re + (p_re * r_re - p_im * r_im), f_im + (p_re * r_im + p_im * r_re))

    row = lax.broadcasted_iota(jnp.int32, (SUBLANES, ns), 0)
    e_re = pltpu.roll(f_re, 1, 0)
    e_im = pltpu.roll(f_im, 1, 0)
    zero = jnp.zeros((SUBLANES, ns), jnp.float32)
    carry_ref[0] = jnp.where(row == 0, e_re, zero)
    carry_ref[1] = jnp.where(row == 0, e_im, zero)
    start = (jnp.where(row == 0, in_re, e_re), jnp.where(row == 0, in_im, e_im))

    def store_pair(i2, s):
        s1 = advance(2 * i2, *s)
        s2 = advance(2 * i2 + 1, *s1)
        r0 = pl.multiple_of(i2 * 2 * SUBLANES, 2 * SUBLANES)
        s_ref[pl.ds(r0, 2 * SUBLANES), 0:ns] = jnp.concatenate([s1[0], s2[0]], axis=0).astype(jnp.bfloat16)
        s_ref[pl.ds(r0, 2 * SUBLANES), ns:2 * ns] = jnp.concatenate([s1[1], s2[1]], axis=0).astype(jnp.bfloat16)
        return s2

    lax.fori_loop(0, TIME_SEG // 2, store_pair, start)

    y = jnp.dot(s_ref[...], cw_ref[0], preferred_element_type=jnp.float32)
    o_ref[...] = jax.nn.gelu(y + d_ref[...] * h_ref[...])


def _ssm_core(h, bw, cw, ap, d_skip):
    ns = SSM_TILE_STATES
    return pl.pallas_call(
        _ssm_kernel,
        grid=(SSM_TILES, SEQ // TIME_BLOCK),
        in_specs=[
            pl.BlockSpec((TIME_BLOCK, SSM_TILE_CH), lambda q, t: (t, q)),
            pl.BlockSpec((1, SSM_TILE_CH, 2 * ns), lambda q, t: (q, 0, 0)),
            pl.BlockSpec((1, 2 * ns, SSM_TILE_CH), lambda q, t: (q, 0, 0)),
            pl.BlockSpec((1, 8, SUBLANES, ns), lambda q, t: (q, 0, 0, 0)),
            pl.BlockSpec((1, SSM_TILE_CH), lambda q, t: (0, q)),
        ],
        out_specs=pl.BlockSpec((TIME_BLOCK, SSM_TILE_CH), lambda q, t: (t, q)),
        out_shape=jax.ShapeDtypeStruct((SEQ, D_MODEL), jnp.float32),
        scratch_shapes=[
            pltpu.VMEM((TIME_BLOCK, 2 * ns), jnp.float32),
            pltpu.VMEM((TIME_BLOCK, 2 * ns), jnp.bfloat16),
            pltpu.VMEM((2, SUBLANES, ns), jnp.float32),
        ],
        compiler_params=_params("parallel", "arbitrary"),
        name="ssm_scan",
    )(h, bw, cw, ap, d_skip)


def _ssm_tables(lam_re, lam_im, log_dt, b_re, b_im, c_re, c_im):
    f32 = jnp.float32
    dt = jnp.exp(log_dt.astype(f32))[:, None]
    lr = lam_re.astype(f32)
    li = lam_im.astype(f32)
    mag = jnp.exp(lr * dt)
    ab_re = mag * jnp.cos(li * dt)
    ab_im = mag * jnp.sin(li * dt)
    nr = ab_re - 1.0
    ni = ab_im
    den = lr * lr + li * li
    k_re = ((nr * lr + ni * li) / den)[..., None]
    k_im = ((ni * lr - nr * li) / den)[..., None]
    br = b_re.astype(f32)
    bi = b_im.astype(f32)
    bb = jnp.stack([k_re * br - k_im * bi, k_re * bi + k_im * br])
    cc = jnp.stack([c_re.astype(f32), -c_im.astype(f32)])

    tg = SSM_TILE_GROUPS
    eye = jnp.eye(tg, dtype=f32)
    bb = bb.reshape(2, SSM_TILES, tg, SSM_STATE, SSM_GROUP)
    cc = cc.reshape(2, SSM_TILES, tg, SSM_GROUP, SSM_STATE)
    bw = jnp.einsum("ab,sqapc->qacsbp", eye, bb).reshape(SSM_TILES, SSM_TILE_CH, 2 * SSM_TILE_STATES)
    cw = jnp.einsum("ab,sqacp->qsbpac", eye, cc).reshape(SSM_TILES, 2 * SSM_TILE_STATES, SSM_TILE_CH)

    def square(z):
        return (z[0] * z[0] - z[1] * z[1], 2.0 * z[0] * z[1])

    a = (ab_re, ab_im)
    p = a
    n = 1
    while n < TIME_SEG:
        p = square(p)
        n *= 2
    assert n == TIME_SEG

    def per_sublane(z):
        z = z.reshape(SSM_TILES, 1, SSM_TILE_STATES)
        return jnp.broadcast_to(z, (SSM_TILES, SUBLANES, SSM_TILE_STATES))

    sub = jnp.arange(SUBLANES)[None, :, None]
    tables = [per_sublane(z) for z in a]
    for shift in (1, 2, 4):
        tables += [jnp.where(sub >= shift, per_sublane(z), 0.0) for z in p]
        p = square(p)
    ap = jnp.stack(tables, axis=1)
    return bw.astype(jnp.bfloat16), cw.astype(jnp.bfloat16), ap


def _ssm_out_kernel(y_ref, wv_ref, wg_ref, x_ref, o_ref, yb_ref):
    @pl.when(pl.program_id(1) == 0)
    def _():
        yb_ref[...] = y_ref[...].astype(jnp.bfloat16)

    y = yb_ref[...]
    val = jnp.dot(y, wv_ref[...], preferred_element_type=jnp.float32)
    gate = jnp.dot(y, wg_ref[...], preferred_element_type=jnp.float32)
    o_ref[...] = x_ref[...] + val * jax.nn.sigmoid(gate)


def _ssm_out(y, w_glu, x):
    n_col = D_MODEL // TN
    return pl.pallas_call(
        _ssm_out_kernel,
        grid=(SEQ // TM, n_col),
        in_specs=[
            pl.BlockSpec((TM, D_MODEL), lambda i, n: (i, 0)),
            pl.BlockSpec((D_MODEL, TN), lambda i, n: (0, n)),
            pl.BlockSpec((D_MODEL, TN), lambda i, n: (0, n + n_col)),
            pl.BlockSpec((TM, TN), lambda i, n: (i, n)),
        ],
        out_specs=pl.BlockSpec((TM, TN), lambda i, n: (i, n)),
        out_shape=jax.ShapeDtypeStruct((SEQ, D_MODEL), jnp.float32),
        scratch_shapes=[pltpu.VMEM((TM, D_MODEL), jnp.bfloat16)],
        compiler_params=_params("parallel", "arbitrary"),
        name="ssm_out",
    )(y, w_glu, w_glu, x)


def _mlp_kernel(x_ref, g_ref, wu_ref, wd_ref, o_ref, h_ref):
    f = pl.program_id(1)

    @pl.when(f == 0)
    def _():
        x = x_ref[...]
        h_ref[...] = _rms(x, g_ref[...]).astype(jnp.bfloat16)
        o_ref[...] = x

    a = jnp.maximum(jnp.dot(h_ref[...], wu_ref[...], preferred_element_type=jnp.float32), 0.0)
    a = (a * a).astype(jnp.bfloat16)
    o_ref[...] += jnp.dot(a, wd_ref[...], preferred_element_type=jnp.float32)


def _mlp(x, g, w_up, w_down):
    return pl.pallas_call(
        _mlp_kernel,
        grid=(SEQ // TM_MLP, D_FF // TF_MLP),
        in_specs=[
            pl.BlockSpec((TM_MLP, D_MODEL), lambda i, f: (i, 0)),
            pl.BlockSpec((1, D_MODEL), lambda i, f: (0, 0)),
            pl.BlockSpec((D_MODEL, TF_MLP), lambda i, f: (0, f)),
            pl.BlockSpec((TF_MLP, D_MODEL), lambda i, f: (f, 0)),
        ],
        out_specs=pl.BlockSpec((TM_MLP, D_MODEL), lambda i, f: (i, 0)),
        out_shape=jax.ShapeDtypeStruct((SEQ, D_MODEL), jnp.float32),
        scratch_shapes=[pltpu.VMEM((TM_MLP, D_MODEL), jnp.bfloat16)],
        compiler_params=_params("parallel", "arbitrary"),
        name="mlp",
    )(x, g, w_up, w_down)


def kernel(x, mix_norm, conv_w_in, conv_b_in, conv_dw, conv_dw_b, conv_ln_g, conv_ln_b, conv_w_out, conv_b_out, ssm_lambda_re, ssm_lambda_im, ssm_log_dt, ssm_b_re, ssm_b_im, ssm_c_re, ssm_c_im, ssm_d, ssm_w_glu, mlp_norm, mlp_w_up, mlp_w_down, final_norm):
    assert x.shape == (1, SEQ, D_MODEL) and x.dtype == jnp.float32
    bf16 = jnp.bfloat16
    n_blocks = SEQ // TIME_BLOCK
    x = x.reshape(n_blocks, SUBLANES, TIME_SEG, D_MODEL).transpose(0, 2, 1, 3).reshape(SEQ, D_MODEL)
    for i in range(DEPTH):
        j = i // 2
        g = mix_norm[i].reshape(1, D_MODEL)
        if i % 2 == 0:
            v = _conv_in(x, g, conv_w_in[j].astype(bf16), conv_b_in[j].reshape(1, 2 * D_MODEL))
            v = _dwconv(v, conv_dw[j], conv_dw_b[j].reshape(1, D_MODEL),
                        conv_ln_g[j].reshape(1, D_MODEL), conv_ln_b[j].reshape(1, D_MODEL))
            x = _conv_out(v, conv_w_out[j].astype(bf16), conv_b_out[j].reshape(1, D_MODEL), x)
        else:
            bw, cw, ap = _ssm_tables(ssm_lambda_re[j], ssm_lambda_im[j], ssm_log_dt[j],
                                     ssm_b_re[j], ssm_b_im[j], ssm_c_re[j], ssm_c_im[j])
            h = _rms_norm(x, g)
            y = _ssm_core(h, bw, cw, ap, ssm_d[j].reshape(1, D_MODEL))
            x = _ssm_out(y, ssm_w_glu[j].astype(bf16), x)
        x = _mlp(x, mlp_norm[i].reshape(1, D_MODEL), mlp_w_up[i].astype(bf16), mlp_w_down[i].astype(bf16))
    x = _rms_norm(x, final_norm.reshape(1, D_MODEL))
    x = x.reshape(n_blocks, TIME_SEG, SUBLANES, D_MODEL).transpose(0, 2, 1, 3)
    return x.reshape(1, SEQ, D_MODEL)
```

```python
import functools

import jax
import jax.numpy as jnp
from jax import lax
from jax.experimental import pallas as pl
from jax.experimental.pallas import tpu as pltpu

D_MODEL = 2048
SEQ = 8192
DEPTH = 4
CONV_WIDTH = 31
SSM_GROUP = 16
SSM_GROUPS = D_MODEL // SSM_GROUP
SSM_STATE = 64
D_FF = 4 * D_MODEL
EPS = 1e-6

SUBLANES = 8
LANES = 128
VMEM_LIMIT_BYTES = 56 * 1024 * 1024

TM = 1024
TN = 512
TM_MLP = 512
TF_MLP = 1024

TIME_BLOCK = 512
TIME_SEG = TIME_BLOCK // SUBLANES

CONV_HALO = 32 * SUBLANES
CONV_ROWS = 32
CONV_COLS = 512
LN_ROWS = 16
LN_UNROLL = 4
SCAN_UNROLL = 8

SSM_TILE_GROUPS = 16
SSM_TILE_CH = SSM_TILE_GROUPS * SSM_GROUP
SSM_TILE_STATES = SSM_TILE_GROUPS * SSM_STATE
SSM_TILES = SSM_GROUPS // SSM_TILE_GROUPS


def _params(*sem):
    return pltpu.CompilerParams(dimension_semantics=sem, vmem_limit_bytes=VMEM_LIMIT_BYTES)


def _rms(x, g):
    ms = jnp.mean(x * x, axis=-1, keepdims=True)
    return x * lax.rsqrt(ms + EPS) * g


def _conv_in_kernel(x_ref, g_ref, wa_ref, wg_ref, ba_ref, bg_ref, o_ref, h_ref):
    @pl.when(pl.program_id(1) == 0)
    def _():
        h_ref[...] = _rms(x_ref[...], g_ref[...]).astype(jnp.bfloat16)

    h = h_ref[...]
    a = jnp.dot(h, wa_ref[...], preferred_element_type=jnp.float32) + ba_ref[...]
    gate = jnp.dot(h, wg_ref[...], preferred_element_type=jnp.float32) + bg_ref[...]
    o_ref[...] = a * jax.nn.sigmoid(gate)


def _conv_in(x, g, w_in, layer, b_in):
    n_col = D_MODEL // TN
    return pl.pallas_call(
        _conv_in_kernel,
        grid=(SEQ // TM, n_col),
        in_specs=[
            pl.BlockSpec((TM, D_MODEL), lambda i, n: (i, 0)),
            pl.BlockSpec((1, D_MODEL), lambda i, n: (0, 0)),
            pl.BlockSpec((None, D_MODEL, TN), lambda i, n: (layer, 0, n)),
            pl.BlockSpec((None, D_MODEL, TN), lambda i, n: (layer, 0, n + n_col)),
            pl.BlockSpec((1, TN), lambda i, n: (0, n)),
            pl.BlockSpec((1, TN), lambda i, n: (0, n + n_col)),
        ],
        out_specs=pl.BlockSpec((TM, TN), lambda i, n: (i, n)),
        out_shape=jax.ShapeDtypeStruct((SEQ, D_MODEL), jnp.float32),
        scratch_shapes=[pltpu.VMEM((TM, D_MODEL), jnp.bfloat16)],
        compiler_params=_params("parallel", "arbitrary"),
        name="conv_in",
    )(x, g, w_in, w_in, b_in, b_in)


def _dwconv_kernel(v_ref, dw_ref, dwb_ref, lg_ref, lb_ref, o_ref, ext_ref, tail_ref, cv_ref):
    @pl.when(pl.program_id(0) == 0)
    def _():
        tail_ref[...] = jnp.zeros(tail_ref.shape, jnp.float32)

    tail_start = TIME_BLOCK - CONV_HALO
    first_segment = lax.broadcasted_iota(jnp.int32, (SUBLANES, D_MODEL), 0) == 0

    def halo_rows(m, carry):
        r0 = pl.multiple_of(m * SUBLANES, SUBLANES)
        cur = pltpu.roll(v_ref[pl.ds(tail_start + r0, SUBLANES), :], 1, 0)
        prev = pltpu.roll(tail_ref[pl.ds(r0, SUBLANES), :], 1, 0)
        ext_ref[pl.ds(r0, SUBLANES), :] = jnp.where(first_segment, prev, cur)
        return carry

    lax.fori_loop(0, CONV_HALO // SUBLANES, halo_rows, 0)
    tail_ref[...] = v_ref[tail_start:, :]
    ext_ref[CONV_HALO:, :] = v_ref[...]
    first_tap = CONV_HALO - (CONV_WIDTH - 1) * SUBLANES

    def conv_rows(r, carry):
        r0 = pl.multiple_of(r * CONV_ROWS, CONV_ROWS)
        for c in range(D_MODEL // CONV_COLS):
            cols = slice(c * CONV_COLS, (c + 1) * CONV_COLS)
            acc = jnp.zeros((CONV_ROWS, CONV_COLS), jnp.float32)
            for k in range(CONV_WIDTH):
                taps = ext_ref[pl.ds(r0 + first_tap + k * SUBLANES, CONV_ROWS), cols]
                w = dw_ref[k * SUBLANES:(k + 1) * SUBLANES, cols]
                acc = acc + taps * jnp.concatenate([w] * (CONV_ROWS // SUBLANES), axis=0)
            cv_ref[pl.ds(r0, CONV_ROWS), cols] = acc + dwb_ref[:, cols]
        return carry

    lax.fori_loop(0, TIME_BLOCK // CONV_ROWS, conv_rows, 0)

    def norm_rows(r, carry):
        r0 = pl.multiple_of(r * LN_ROWS, LN_ROWS)
        c = cv_ref[pl.ds(r0, LN_ROWS), :]
        mu = jnp.mean(c, axis=-1, keepdims=True)
        xc = c - mu
        var = jnp.mean(xc * xc, axis=-1, keepdims=True)
        y = xc * lax.rsqrt(var + EPS) * lg_ref[...] + lb_ref[...]
        o_ref[pl.ds(r0, LN_ROWS), :] = (y * jax.nn.sigmoid(y)).astype(jnp.bfloat16)
        return carry

    lax.fori_loop(0, TIME_BLOCK // LN_ROWS, norm_rows, 0, unroll=LN_UNROLL)


def _dwconv(v, dw, dw_b, ln_g, ln_b):
    row = lambda i: (0, 0)
    dw = jnp.repeat(dw, SUBLANES, axis=0)
    return pl.pallas_call(
        _dwconv_kernel,
        grid=(SEQ // TIME_BLOCK,),
        in_specs=[
            pl.BlockSpec((TIME_BLOCK, D_MODEL), lambda i: (i, 0)),
            pl.BlockSpec((CONV_WIDTH * SUBLANES, D_MODEL), row),
            pl.BlockSpec((1, D_MODEL), row),
            pl.BlockSpec((1, D_MODEL), row),
            pl.BlockSpec((1, D_MODEL), row),
        ],
        out_specs=pl.BlockSpec((TIME_BLOCK, D_MODEL), lambda i: (i, 0)),
        out_shape=jax.ShapeDtypeStruct((SEQ, D_MODEL), jnp.bfloat16),
        scratch_shapes=[
            pltpu.VMEM((CONV_HALO + TIME_BLOCK, D_MODEL), jnp.float32),
            pltpu.VMEM((CONV_HALO, D_MODEL), jnp.float32),
            pltpu.VMEM((TIME_BLOCK, D_MODEL), jnp.float32),
        ],
        compiler_params=_params("arbitrary"),
        name="dwconv_ln_silu",
    )(v, dw, dw_b, ln_g, ln_b)


def _conv_out_kernel(v_ref, w_ref, b_ref, x_ref, o_ref):
    y = jnp.dot(v_ref[...], w_ref[...], preferred_element_type=jnp.float32)
    o_ref[...] = x_ref[...] + (y + b_ref[...])


def _conv_out(v, w_out, layer, b_out, x):
    return pl.pallas_call(
        _conv_out_kernel,
        grid=(SEQ // TM, D_MODEL // TN),
        in_specs=[
            pl.BlockSpec((TM, D_MODEL), lambda i, n: (i, 0)),
            pl.BlockSpec((None, D_MODEL, TN), lambda i, n: (layer, 0, n)),
            pl.BlockSpec((1, TN), lambda i, n: (0, n)),
            pl.BlockSpec((TM, TN), lambda i, n: (i, n)),
        ],
        out_specs=pl.BlockSpec((TM, TN), lambda i, n: (i, n)),
        out_shape=jax.ShapeDtypeStruct((SEQ, D_MODEL), jnp.float32),
        compiler_params=_params("parallel", "parallel"),
        name="conv_out",
    )(v, w_out, b_out, x)


def _rms_kernel(x_ref, g_ref, o_ref):
    o_ref[...] = _rms(x_ref[...], g_ref[...])


def _rms_norm(x, g):
    return pl.pallas_call(
        _rms_kernel,
        grid=(SEQ // TM,),
        in_specs=[
            pl.BlockSpec((TM, D_MODEL), lambda i: (i, 0)),
            pl.BlockSpec((1, D_MODEL), lambda i: (0, 0)),
        ],
        out_specs=pl.BlockSpec((TM, D_MODEL), lambda i: (i, 0)),
        out_shape=jax.ShapeDtypeStruct((SEQ, D_MODEL), jnp.float32),
        compiler_params=_params("parallel"),
        name="rms_norm",
    )(x, g)


def _ssm_kernel(h_ref, bw_ref, cw_ref, ap_ref, d_ref, o_ref, bu_ref, s_ref, carry_ref):
    ns = SSM_TILE_STATES

    @pl.when(pl.program_id(1) == 0)
    def _():
        carry_ref[...] = jnp.zeros(carry_ref.shape, jnp.float32)

    bu_ref[...] = jnp.dot(h_ref[...].astype(jnp.bfloat16), bw_ref[0], preferred_element_type=jnp.float32)

    a_re = ap_ref[0, 0]
    a_im = ap_ref[0, 1]

    def advance(i, s_re, s_im):
        r0 = pl.multiple_of(i * SUBLANES, SUBLANES)
        b_re = bu_ref[pl.ds(r0, SUBLANES), 0:ns]
        b_im = bu_ref[pl.ds(r0, SUBLANES), ns:2 * ns]
        return (a_re * s_re - a_im * s_im + b_re, a_re * s_im + a_im * s_re + b_im)

    in_re = carry_ref[0]
    in_im = carry_ref[1]
    f_re, f_im = lax.fori_loop(0, TIME_SEG, lambda i, s: advance(i, *s), (in_re, in_im),
                               unroll=SCAN_UNROLL)

    for level, shift in enumerate((1, 2, 4)):
        p_re = ap_ref[0, 2 + 2 * level]
        p_im = ap_ref[0, 3 + 2 * level]
        r_re = pltpu.roll(f_re, shift, 0)
        r_im = pltpu.roll(f_im, shift, 0)
        f_re, f_im = (f_re + (p_re * r_re - p_im * r_im), f_im + (p_re * r_im + p_im * r_re))

    row = lax.broadcasted_iota(jnp.int32, (SUBLANES, ns), 0)
    e_re = pltpu.roll(f_re, 1, 0)
    e_im = pltpu.roll(f_im, 1, 0)
    zero = jnp.zeros((SUBLANES, ns), jnp.float32)
    carry_ref[0] = jnp.where(row == 0, e_re, zero)
    carry_ref[1] = jnp.where(row == 0, e_im, zero)
    start = (jnp.where(row == 0, in_re, e_re), jnp.where(row == 0, in_im, e_im))

    def store_pair(i2, s):
        s1 = advance(2 * i2, *s)
        s2 = advance(2 * i2 + 1, *s1)
        r0 = pl.multiple_of(i2 * 2 * SUBLANES, 2 * SUBLANES)
        s_ref[pl.ds(r0, 2 * SUBLANES), 0:ns] = jnp.concatenate([s1[0], s2[0]], axis=0).astype(jnp.bfloat16)
        s_ref[pl.ds(r0, 2 * SUBLANES), ns:2 * ns] = jnp.concatenate([s1[1], s2[1]], axis=0).astype(jnp.bfloat16)
        return s2

    lax.fori_loop(0, TIME_SEG // 2, store_pair, start, unroll=SCAN_UNROLL // 2)

    y = jnp.dot(s_ref[...], cw_ref[0], preferred_element_type=jnp.float32)
    o_ref[...] = jax.nn.gelu(y + d_ref[...] * h_ref[...])


def _ssm_core(h, bw, cw, ap, d_skip):
    ns = SSM_TILE_STATES
    return pl.pallas_call(
        _ssm_kernel,
        grid=(SSM_TILES, SEQ // TIME_BLOCK),
        in_specs=[
            pl.BlockSpec((TIME_BLOCK, SSM_TILE_CH), lambda q, t: (t, q)),
            pl.BlockSpec((1, SSM_TILE_CH, 2 * ns), lambda q, t: (q, 0, 0)),
            pl.BlockSpec((1, 2 * ns, SSM_TILE_CH), lambda q, t: (q, 0, 0)),
            pl.BlockSpec((1, 8, SUBLANES, ns), lambda q, t: (q, 0, 0, 0)),
            pl.BlockSpec((1, SSM_TILE_CH), lambda q, t: (0, q)),
        ],
        out_specs=pl.BlockSpec((TIME_BLOCK, SSM_TILE_CH), lambda q, t: (t, q)),
        out_shape=jax.ShapeDtypeStruct((SEQ, D_MODEL), jnp.float32),
        scratch_shapes=[
            pltpu.VMEM((TIME_BLOCK, 2 * ns), jnp.float32),
            pltpu.VMEM((TIME_BLOCK, 2 * ns), jnp.bfloat16),
            pltpu.VMEM((2, SUBLANES, ns), jnp.float32),
        ],
        compiler_params=_params("parallel", "arbitrary"),
        name="ssm_scan",
    )(h, bw, cw, ap, d_skip)


def _ssm_tables(lam_re, lam_im, log_dt, b_re, b_im, c_re, c_im):
    f32 = jnp.float32
    dt = jnp.exp(log_dt.astype(f32))[:, None]
    lr = lam_re.astype(f32)
    li = lam_im.astype(f32)
    mag = jnp.exp(lr * dt)
    ab_re = mag * jnp.cos(li * dt)
    ab_im = mag * jnp.sin(li * dt)
    nr = ab_re - 1.0
    ni = ab_im
    den = lr * lr + li * li
    k_re = ((nr * lr + ni * li) / den)[..., None]
    k_im = ((ni * lr - nr * li) / den)[..., None]
    br = b_re.astype(f32)
    bi = b_im.astype(f32)
    bb = jnp.stack([k_re * br - k_im * bi, k_re * bi + k_im * br])
    cc = jnp.stack([c_re.astype(f32), -c_im.astype(f32)])

    tg = SSM_TILE_GROUPS
    eye = jnp.eye(tg, dtype=f32)
    bb = bb.reshape(2, SSM_TILES, tg, SSM_STATE, SSM_GROUP)
    cc = cc.reshape(2, SSM_TILES, tg, SSM_GROUP, SSM_STATE)
    bw = jnp.einsum("ab,sqapc->qacsbp", eye, bb).reshape(SSM_TILES, SSM_TILE_CH, 2 * SSM_TILE_STATES)
    cw = jnp.einsum("ab,sqacp->qsbpac", eye, cc).reshape(SSM_TILES, 2 * SSM_TILE_STATES, SSM_TILE_CH)

    def square(z):
        return (z[0] * z[0] - z[1] * z[1], 2.0 * z[0] * z[1])

    a = (ab_re, ab_im)
    p = a
    n = 1
    while n < TIME_SEG:
        p = square(p)
        n *= 2
    assert n == TIME_SEG

    def per_sublane(z):
        z = z.reshape(SSM_TILES, 1, SSM_TILE_STATES)
        return jnp.broadcast_to(z, (SSM_TILES, SUBLANES, SSM_TILE_STATES))

    sub = jnp.arange(SUBLANES)[None, :, None]
    tables = [per_sublane(z) for z in a]
    for shift in (1, 2, 4):
        tables += [jnp.where(sub >= shift, per_sublane(z), 0.0) for z in p]
        p = square(p)
    ap = jnp.stack(tables, axis=1)
    return bw.astype(jnp.bfloat16), cw.astype(jnp.bfloat16), ap


def _ssm_out_kernel(y_ref, wv_ref, wg_ref, x_ref, o_ref, yb_ref):
    @pl.when(pl.program_id(1) == 0)
    def _():
        yb_ref[...] = y_ref[...].astype(jnp.bfloat16)

    y = yb_ref[...]
    val = jnp.dot(y, wv_ref[...], preferred_element_type=jnp.float32)
    gate = jnp.dot(y, wg_ref[...], preferred_element_type=jnp.float32)
    o_ref[...] = x_ref[...] + val * jax.nn.sigmoid(gate)


def _ssm_out(y, w_glu, layer, x):
    n_col = D_MODEL // TN
    return pl.pallas_call(
        _ssm_out_kernel,
        grid=(SEQ // TM, n_col),
        in_specs=[
            pl.BlockSpec((TM, D_MODEL), lambda i, n: (i, 0)),
            pl.BlockSpec((None, D_MODEL, TN), lambda i, n: (layer, 0, n)),
            pl.BlockSpec((None, D_MODEL, TN), lambda i, n: (layer, 0, n + n_col)),
            pl.BlockSpec((TM, TN), lambda i, n: (i, n)),
        ],
        out_specs=pl.BlockSpec((TM, TN), lambda i, n: (i, n)),
        out_shape=jax.ShapeDtypeStruct((SEQ, D_MODEL), jnp.float32),
        scratch_shapes=[pltpu.VMEM((TM, D_MODEL), jnp.bfloat16)],
        compiler_params=_params("parallel", "arbitrary"),
        name="ssm_out",
    )(y, w_glu, w_glu, x)


def _mlp_kernel(x_ref, g_ref, wu_ref, wd_ref, o_ref, h_ref):
    f = pl.program_id(1)

    @pl.when(f == 0)
    def _():
        x = x_ref[...]
        h_ref[...] = _rms(x, g_ref[...]).astype(jnp.bfloat16)
        o_ref[...] = x

    a = jnp.maximum(jnp.dot(h_ref[...], wu_ref[...], preferred_element_type=jnp.float32), 0.0)
    a = (a * a).astype(jnp.bfloat16)
    o_ref[...] += jnp.dot(a, wd_ref[...], preferred_element_type=jnp.float32)


def _mlp(x, g, w_up, w_down, layer):
    return pl.pallas_call(
        _mlp_kernel,
        grid=(SEQ // TM_MLP, D_FF // TF_MLP),
        in_specs=[
            pl.BlockSpec((TM_MLP, D_MODEL), lambda i, f: (i, 0)),
            pl.BlockSpec((1, D_MODEL), lambda i, f: (0, 0)),
            pl.BlockSpec((None, D_MODEL, TF_MLP), lambda i, f: (layer, 0, f)),
            pl.BlockSpec((None, TF_MLP, D_MODEL), lambda i, f: (layer, f, 0)),
        ],
        out_specs=pl.BlockSpec((TM_MLP, D_MODEL), lambda i, f: (i, 0)),
        out_shape=jax.ShapeDtypeStruct((SEQ, D_MODEL), jnp.float32),
        scratch_shapes=[pltpu.VMEM((TM_MLP, D_MODEL), jnp.bfloat16)],
        compiler_params=_params("parallel", "arbitrary"),
        name="mlp",
    )(x, g, w_up, w_down)


def kernel(x, mix_norm, conv_w_in, conv_b_in, conv_dw, conv_dw_b, conv_ln_g, conv_ln_b, conv_w_out, conv_b_out, ssm_lambda_re, ssm_lambda_im, ssm_log_dt, ssm_b_re, ssm_b_im, ssm_c_re, ssm_c_im, ssm_d, ssm_w_glu, mlp_norm, mlp_w_up, mlp_w_down, final_norm):
    assert x.shape == (1, SEQ, D_MODEL) and x.dtype == jnp.float32
    bf16 = jnp.bfloat16
    n_blocks = SEQ // TIME_BLOCK
    x = x.reshape(n_blocks, SUBLANES, TIME_SEG, D_MODEL).transpose(0, 2, 1, 3).reshape(SEQ, D_MODEL)
    conv_w_in, conv_w_out, ssm_w_glu = conv_w_in.astype(bf16), conv_w_out.astype(bf16), ssm_w_glu.astype(bf16)
    mlp_w_up, mlp_w_down = mlp_w_up.astype(bf16), mlp_w_down.astype(bf16)
    for i in range(DEPTH):
        j = i // 2
        g = mix_norm[i].reshape(1, D_MODEL)
        if i % 2 == 0:
            v = _conv_in(x, g, conv_w_in, j, conv_b_in[j].reshape(1, 2 * D_MODEL))
            v = _dwconv(v, conv_dw[j], conv_dw_b[j].reshape(1, D_MODEL),
                        conv_ln_g[j].reshape(1, D_MODEL), conv_ln_b[j].reshape(1, D_MODEL))
            x = _conv_out(v, conv_w_out, j, conv_b_out[j].reshape(1, D_MODEL), x)
        else:
            bw, cw, ap = _ssm_tables(ssm_lambda_re[j], ssm_lambda_im[j], ssm_log_dt[j],
                                     ssm_b_re[j], ssm_b_im[j], ssm_c_re[j], ssm_c_im[j])
            h = _rms_norm(x, g)
            y = _ssm_core(h, bw, cw, ap, ssm_d[j].reshape(1, D_MODEL))
            x = _ssm_out(y, ssm_w_glu, j, x)
        x = _mlp(x, mlp_norm[i].reshape(1, D_MODEL), mlp_w_up, mlp_w_down, i)
    x = _rms_norm(x, final_norm.reshape(1, D_MODEL))
    x = x.reshape(n_blocks, TIME_SEG, SUBLANES, D_MODEL).transpose(0, 2, 1, 3)
    return x.reshape(1, SEQ, D_MODEL)
```

```python
import functools

import jax
import jax.numpy as jnp
from jax import lax
from jax.experimental import pallas as pl
from jax.experimental.pallas import tpu as pltpu

D_MODEL = 2048
SEQ = 8192
DEPTH = 4
CONV_WIDTH = 31
SSM_GROUP = 16
SSM_GROUPS = D_MODEL // SSM_GROUP
SSM_STATE = 64
D_FF = 4 * D_MODEL
EPS = 1e-6

SUBLANES = 8
LANES = 128
VMEM_LIMIT_BYTES = 56 * 1024 * 1024

TM = 1024
TN = 512
TM_MLP = 512
TF_MLP = 1024

TIME_BLOCK = 512
TIME_SEG = TIME_BLOCK // SUBLANES

CONV_HALO = 32 * SUBLANES
CONV_ROWS = 128
LN_ROWS = 16
LN_UNROLL = 4

SSM_TILE_GROUPS = 16
SSM_TILE_CH = SSM_TILE_GROUPS * SSM_GROUP
SSM_TILE_STATES = SSM_TILE_GROUPS * SSM_STATE
SSM_TILES = SSM_GROUPS // SSM_TILE_GROUPS
SSM_IN_COLS = 256


def _params(*sem):
    return pltpu.CompilerParams(dimension_semantics=sem, vmem_limit_bytes=VMEM_LIMIT_BYTES)


def _rms(x, g):
    ms = jnp.mean(x * x, axis=-1, keepdims=True)
    return x * lax.rsqrt(ms + EPS) * g


def _conv_in_kernel(x_ref, g_ref, wa_ref, wg_ref, ba_ref, bg_ref, o_ref, h_ref):
    @pl.when(pl.program_id(1) == 0)
    def _():
        h_ref[...] = _rms(x_ref[...], g_ref[...]).astype(jnp.bfloat16)

    h = h_ref[...]
    a = jnp.dot(h, wa_ref[...], preferred_element_type=jnp.float32) + ba_ref[...]
    gate = jnp.dot(h, wg_ref[...], preferred_element_type=jnp.float32) + bg_ref[...]
    v = a * jax.nn.sigmoid(gate)
    for j in range(TN // LANES):
        o_ref[j] = v[:, j * LANES:(j + 1) * LANES]


def _conv_in(x, g, w_in, layer, b_in):
    n_col = D_MODEL // TN
    return pl.pallas_call(
        _conv_in_kernel,
        grid=(SEQ // TM, n_col),
        in_specs=[
            pl.BlockSpec((TM, D_MODEL), lambda i, n: (i, 0)),
            pl.BlockSpec((1, D_MODEL), lambda i, n: (0, 0)),
            pl.BlockSpec((None, D_MODEL, TN), lambda i, n: (layer, 0, n)),
            pl.BlockSpec((None, D_MODEL, TN), lambda i, n: (layer, 0, n + n_col)),
            pl.BlockSpec((1, TN), lambda i, n: (0, n)),
            pl.BlockSpec((1, TN), lambda i, n: (0, n + n_col)),
        ],
        out_specs=pl.BlockSpec((TN // LANES, TM, LANES), lambda i, n: (n, i, 0)),
        out_shape=jax.ShapeDtypeStruct((D_MODEL // LANES, SEQ, LANES), jnp.float32),
        scratch_shapes=[pltpu.VMEM((TM, D_MODEL), jnp.bfloat16)],
        compiler_params=_params("parallel", "arbitrary"),
        name="conv_in",
    )(x, g, w_in, w_in, b_in, b_in)


def _dwconv_kernel(v_ref, dw_ref, dwb_ref, lg_ref, lb_ref, o_ref, ext_ref, tail_ref, cv_ref):
    @pl.when(pl.program_id(0) == 0)
    def _():
        tail_ref[...] = jnp.zeros(tail_ref.shape, jnp.float32)

    tail_start = TIME_BLOCK - CONV_HALO
    first_tap = CONV_HALO - (CONV_WIDTH - 1) * SUBLANES
    first_segment = lax.broadcasted_iota(jnp.int32, (SUBLANES, LANES), 0) == 0

    def conv_lane_tile(c, carry):
        ext_ref[c, CONV_HALO:, :] = v_ref[c]
        for m in range(CONV_HALO // SUBLANES):
            rows = slice(m * SUBLANES, (m + 1) * SUBLANES)
            cur = pltpu.roll(v_ref[c, tail_start + m * SUBLANES:tail_start + (m + 1) * SUBLANES, :], 1, 0)
            prev = pltpu.roll(tail_ref[c, rows, :], 1, 0)
            ext_ref[c, rows, :] = jnp.where(first_segment, prev, cur)
        tail_ref[c] = v_ref[c, tail_start:, :]
        for r in range(TIME_BLOCK // CONV_ROWS):
            acc = jnp.zeros((CONV_ROWS, LANES), jnp.float32)
            for k in range(CONV_WIDTH):
                start = r * CONV_ROWS + first_tap + k * SUBLANES
                w = dw_ref[c, k * SUBLANES:(k + 1) * SUBLANES, :]
                acc = acc + ext_ref[c, start:start + CONV_ROWS, :] * jnp.concatenate(
                    [w] * (CONV_ROWS // SUBLANES), axis=0)
            cv_ref[c, r * CONV_ROWS:(r + 1) * CONV_ROWS, :] = acc + dwb_ref[c]
        return carry

    lax.fori_loop(0, D_MODEL // LANES, conv_lane_tile, 0)

    def norm_rows(r, carry):
        r0 = pl.multiple_of(r * LN_ROWS, LN_ROWS)
        c = cv_ref[:, pl.ds(r0, LN_ROWS), :]
        mu = jnp.sum(jnp.sum(c, axis=0), axis=-1, keepdims=True) * (1.0 / D_MODEL)
        xc = c - mu
        var = jnp.sum(jnp.sum(xc * xc, axis=0), axis=-1, keepdims=True) * (1.0 / D_MODEL)
        y = xc * lax.rsqrt(var + EPS) * lg_ref[...] + lb_ref[...]
        y = (y * jax.nn.sigmoid(y)).astype(jnp.bfloat16)
        for j in range(D_MODEL // LANES):
            o_ref[pl.ds(r0, LN_ROWS), j * LANES:(j + 1) * LANES] = y[j]
        return carry

    lax.fori_loop(0, TIME_BLOCK // LN_ROWS, norm_rows, 0, unroll=LN_UNROLL)


def _lane_tiles(p):
    p = p.reshape(-1, D_MODEL // LANES, LANES)
    return p.transpose(1, 0, 2)


def _dwconv(v, dw, dw_b, ln_g, ln_b):
    n_lt = D_MODEL // LANES
    whole = lambda i: (0, 0, 0)
    dw = _lane_tiles(jnp.repeat(dw, SUBLANES, axis=0))
    return pl.pallas_call(
        _dwconv_kernel,
        grid=(SEQ // TIME_BLOCK,),
        in_specs=[
            pl.BlockSpec((n_lt, TIME_BLOCK, LANES), lambda i: (0, i, 0)),
            pl.BlockSpec((n_lt, CONV_WIDTH * SUBLANES, LANES), whole),
            pl.BlockSpec((n_lt, 1, LANES), whole),
            pl.BlockSpec((n_lt, 1, LANES), whole),
            pl.BlockSpec((n_lt, 1, LANES), whole),
        ],
        out_specs=pl.BlockSpec((TIME_BLOCK, D_MODEL), lambda i: (i, 0)),
        out_shape=jax.ShapeDtypeStruct((SEQ, D_MODEL), jnp.bfloat16),
        scratch_shapes=[
            pltpu.VMEM((n_lt, CONV_HALO + TIME_BLOCK, LANES), jnp.float32),
            pltpu.VMEM((n_lt, CONV_HALO, LANES), jnp.float32),
            pltpu.VMEM((n_lt, TIME_BLOCK, LANES), jnp.float32),
        ],
        compiler_params=_params("arbitrary"),
        name="dwconv_ln_silu",
    )(v, dw, _lane_tiles(dw_b), _lane_tiles(ln_g), _lane_tiles(ln_b))


def _conv_out_kernel(v_ref, w_ref, b_ref, x_ref, o_ref):
    y = jnp.dot(v_ref[...], w_ref[...], preferred_element_type=jnp.float32)
    o_ref[...] = x_ref[...] + (y + b_ref[...])


def _conv_out(v, w_out, layer, b_out, x):
    return pl.pallas_call(
        _conv_out_kernel,
        grid=(SEQ // TM, D_MODEL // TN),
        in_specs=[
            pl.BlockSpec((TM, D_MODEL), lambda i, n: (i, 0)),
            pl.BlockSpec((None, D_MODEL, TN), lambda i, n: (layer, 0, n)),
            pl.BlockSpec((1, TN), lambda i, n: (0, n)),
            pl.BlockSpec((TM, TN), lambda i, n: (i, n)),
        ],
        out_specs=pl.BlockSpec((TM, TN), lambda i, n: (i, n)),
        out_shape=jax.ShapeDtypeStruct((SEQ, D_MODEL), jnp.float32),
        compiler_params=_params("parallel", "parallel"),
        name="conv_out",
    )(v, w_out, b_out, x)


def _rms_kernel(x_ref, g_ref, o_ref):
    o_ref[...] = _rms(x_ref[...], g_ref[...])


def _rms_norm(x, g):
    return pl.pallas_call(
        _rms_kernel,
        grid=(SEQ // TM,),
        in_specs=[
            pl.BlockSpec((TM, D_MODEL), lambda i: (i, 0)),
            pl.BlockSpec((1, D_MODEL), lambda i: (0, 0)),
        ],
        out_specs=pl.BlockSpec((TM, D_MODEL), lambda i: (i, 0)),
        out_shape=jax.ShapeDtypeStruct((SEQ, D_MODEL), jnp.float32),
        compiler_params=_params("parallel"),
        name="rms_norm",
    )(x, g)


def _rms_pair_kernel(x_ref, g_ref, o_ref, ob_ref):
    h = _rms(x_ref[...], g_ref[...])
    o_ref[...] = h
    ob_ref[...] = h.astype(jnp.bfloat16)


def _rms_norm_pair(x, g):
    rows = pl.BlockSpec((TM, D_MODEL), lambda i: (i, 0))
    return pl.pallas_call(
        _rms_pair_kernel,
        grid=(SEQ // TM,),
        in_specs=[rows, pl.BlockSpec((1, D_MODEL), lambda i: (0, 0))],
        out_specs=[rows, rows],
        out_shape=[jax.ShapeDtypeStruct((SEQ, D_MODEL), jnp.float32),
                   jax.ShapeDtypeStruct((SEQ, D_MODEL), jnp.bfloat16)],
        compiler_params=_params("parallel"),
        name="rms_norm_pair",
    )(x, g)


def _ssm_stages(h_in_ref, h_out_ref, bw_ref, cw_ref, ap_ref, d_ref, o_ref, carry_ref, keep_carry,
                bu_w_ref, bu_r_ref, s_w_ref, s_r_ref):
    ns = SSM_TILE_STATES

    def project_in(n):
        cols = slice(n * SSM_IN_COLS, (n + 1) * SSM_IN_COLS)
        bu_w_ref[:, cols] = jnp.dot(h_in_ref[...], bw_ref[0, :, cols], preferred_element_type=jnp.float32)

    a_re = ap_ref[0, 0]
    a_im = ap_ref[0, 1]

    def advance(i, s_re, s_im):
        rows = slice(i * SUBLANES, (i + 1) * SUBLANES)
        b_re = bu_r_ref[rows, 0:ns]
        b_im = bu_r_ref[rows, ns:2 * ns]
        return (a_re * s_re - a_im * s_im + b_re, a_re * s_im + a_im * s_re + b_im)

    zero = jnp.zeros((SUBLANES, ns), jnp.float32)
    in_re = jnp.where(keep_carry, carry_ref[0], zero)
    in_im = jnp.where(keep_carry, carry_ref[1], zero)
    f_re, f_im = in_re, in_im
    n_in = 2 * ns // SSM_IN_COLS
    for i in range(TIME_SEG):
        f_re, f_im = advance(i, f_re, f_im)
        if i % (TIME_SEG // n_in) == 0:
            project_in(i // (TIME_SEG // n_in))

    for level, shift in enumerate((1, 2, 4)):
        p_re = ap_ref[0, 2 + 2 * level]
        p_im = ap_ref[0, 3 + 2 * level]
        r_re = pltpu.roll(f_re, shift, 0)
        r_im = pltpu.roll(f_im, shift, 0)
        f_re, f_im = (f_re + (p_re * r_re - p_im * r_im), f_im + (p_re * r_im + p_im * r_re))

    row = lax.broadcasted_iota(jnp.int32, (SUBLANES, ns), 0)
    e_re = pltpu.roll(f_re, 1, 0)
    e_im = pltpu.roll(f_im, 1, 0)
    carry_ref[0] = jnp.where(row == 0, e_re, zero)
    carry_ref[1] = jnp.where(row == 0, e_im, zero)
    s_re = jnp.where(row == 0, in_re, e_re)
    s_im = jnp.where(row == 0, in_im, e_im)

    y = jnp.dot(s_r_ref[...], cw_ref[0], preferred_element_type=jnp.float32)
    o_ref[...] = y + d_ref[...] * h_out_ref[...]

    for i2 in range(TIME_SEG // 2):
        m_re, m_im = advance(2 * i2, s_re, s_im)
        s_re, s_im = advance(2 * i2 + 1, m_re, m_im)
        rows = slice(i2 * 2 * SUBLANES, (i2 + 1) * 2 * SUBLANES)
        s_w_ref[rows, 0:ns] = jnp.concatenate([m_re, s_re], axis=0).astype(jnp.bfloat16)
        s_w_ref[rows, ns:2 * ns] = jnp.concatenate([m_im, s_im], axis=0).astype(jnp.bfloat16)


def _ssm_kernel(h_in_ref, h_out_ref, bw_ref, cw_ref, ap_ref, d_ref, o_ref,
                bu0_ref, bu1_ref, s0_ref, s1_ref, carry_ref):
    g = pl.program_id(0)

    @pl.when(g == 0)
    def _():
        for ref in (bu0_ref, bu1_ref, s0_ref, s1_ref, carry_ref):
            ref[...] = jnp.zeros(ref.shape, ref.dtype)

    keep_carry = g % (SEQ // TIME_BLOCK) != 1
    stages = functools.partial(_ssm_stages, h_in_ref, h_out_ref, bw_ref, cw_ref, ap_ref, d_ref, o_ref,
                               carry_ref, keep_carry)

    @pl.when(g % 2 == 0)
    def _():
        stages(bu0_ref, bu1_ref, s1_ref, s0_ref)

    @pl.when(g % 2 == 1)
    def _():
        stages(bu1_ref, bu0_ref, s0_ref, s1_ref)


def _ssm_core(h, h_bf16, bw, cw, ap, d_skip):
    ns = SSM_TILE_STATES
    n_time = SEQ // TIME_BLOCK
    n_blocks = SSM_TILES * n_time

    def block(g, lag):
        b = jnp.clip(g - lag, 0, n_blocks - 1)
        return b % n_time, b // n_time

    def h_map(lag):
        return lambda g: block(g, lag)

    def tile_map(lag, rank):
        return lambda g: (block(g, lag)[1],) + (0,) * (rank - 1)

    return pl.pallas_call(
        _ssm_kernel,
        grid=(n_blocks + 2,),
        in_specs=[
            pl.BlockSpec((TIME_BLOCK, SSM_TILE_CH), h_map(0)),
            pl.BlockSpec((TIME_BLOCK, SSM_TILE_CH), h_map(2)),
            pl.BlockSpec((1, SSM_TILE_CH, 2 * ns), tile_map(0, 3)),
            pl.BlockSpec((1, 2 * ns, SSM_TILE_CH), tile_map(2, 3)),
            pl.BlockSpec((1, 8, SUBLANES, ns), tile_map(1, 4)),
            pl.BlockSpec((1, SSM_TILE_CH), lambda g: (0, block(g, 2)[1])),
        ],
        out_specs=pl.BlockSpec((TIME_BLOCK, SSM_TILE_CH), h_map(2)),
        out_shape=jax.ShapeDtypeStruct((SEQ, D_MODEL), jnp.float32),
        scratch_shapes=[
            pltpu.VMEM((TIME_BLOCK, 2 * ns), jnp.float32),
            pltpu.VMEM((TIME_BLOCK, 2 * ns), jnp.float32),
            pltpu.VMEM((TIME_BLOCK, 2 * ns), jnp.bfloat16),
            pltpu.VMEM((TIME_BLOCK, 2 * ns), jnp.bfloat16),
            pltpu.VMEM((2, SUBLANES, ns), jnp.float32),
        ],
        compiler_params=_params("arbitrary"),
        name="ssm_scan",
    )(h_bf16, h, bw, cw, ap, d_skip)


def _ssm_tables(lam_re, lam_im, log_dt, b_re, b_im, c_re, c_im):
    f32 = jnp.float32
    dt = jnp.exp(log_dt.astype(f32))[:, None]
    lr = lam_re.astype(f32)
    li = lam_im.astype(f32)
    mag = jnp.exp(lr * dt)
    ab_re = mag * jnp.cos(li * dt)
    ab_im = mag * jnp.sin(li * dt)
    nr = ab_re - 1.0
    ni = ab_im
    den = lr * lr + li * li
    k_re = ((nr * lr + ni * li) / den)[..., None]
    k_im = ((ni * lr - nr * li) / den)[..., None]
    br = b_re.astype(f32)
    bi = b_im.astype(f32)
    bb = jnp.stack([k_re * br - k_im * bi, k_re * bi + k_im * br])
    cc = jnp.stack([c_re.astype(f32), -c_im.astype(f32)])

    tg = SSM_TILE_GROUPS
    ns2 = 2 * SSM_TILE_STATES
    bb = bb.reshape(2, SSM_TILES, tg, SSM_STATE, SSM_GROUP).transpose(1, 4, 0, 2, 3)
    cc = cc.reshape(2, SSM_TILES, tg, SSM_GROUP, SSM_STATE).transpose(1, 0, 2, 4, 3)
    bb = jnp.tile(bb.reshape(SSM_TILES, SSM_GROUP, ns2), (1, tg, 1))
    cc = jnp.tile(cc.reshape(SSM_TILES, ns2, SSM_GROUP), (1, 1, tg))
    ch_group = jnp.arange(SSM_TILE_CH) // SSM_GROUP
    col_group = (jnp.arange(ns2) % SSM_TILE_STATES) // SSM_STATE
    bw = jnp.where(ch_group[:, None] == col_group[None, :], bb, 0.0)
    cw = jnp.where(col_group[:, None] == ch_group[None, :], cc, 0.0)

    def square(z):
        return (z[0] * z[0] - z[1] * z[1], 2.0 * z[0] * z[1])

    a = (ab_re, ab_im)
    p = a
    n = 1
    while n < TIME_SEG:
        p = square(p)
        n *= 2
    assert n == TIME_SEG

    def per_sublane(z):
        z = z.reshape(SSM_TILES, 1, SSM_TILE_STATES)
        return jnp.broadcast_to(z, (SSM_TILES, SUBLANES, SSM_TILE_STATES))

    sub = jnp.arange(SUBLANES)[None, :, None]
    tables = [per_sublane(z) for z in a]
    for shift in (1, 2, 4):
        tables += [jnp.where(sub >= shift, per_sublane(z), 0.0) for z in p]
        p = square(p)
    ap = jnp.stack(tables, axis=1)
    return bw.astype(jnp.bfloat16), cw.astype(jnp.bfloat16), ap


def _ssm_out_kernel(y_ref, wv_ref, wg_ref, x_ref, o_ref, yb_ref):
    @pl.when(pl.program_id(1) == 0)
    def _():
        yb_ref[...] = jax.nn.gelu(y_ref[...]).astype(jnp.bfloat16)

    y = yb_ref[...]
    val = jnp.dot(y, wv_ref[...], preferred_element_type=jnp.float32)
    gate = jnp.dot(y, wg_ref[...], preferred_element_type=jnp.float32)
    o_ref[...] = x_ref[...] + val * jax.nn.sigmoid(gate)


def _ssm_out(y, w_glu, layer, x):
    n_col = D_MODEL // TN
    return pl.pallas_call(
        _ssm_out_kernel,
        grid=(SEQ // TM, n_col),
        in_specs=[
            pl.BlockSpec((TM, D_MODEL), lambda i, n: (i, 0)),
            pl.BlockSpec((None, D_MODEL, TN), lambda i, n: (layer, 0, n)),
            pl.BlockSpec((None, D_MODEL, TN), lambda i, n: (layer, 0, n + n_col)),
            pl.BlockSpec((TM, TN), lambda i, n: (i, n)),
        ],
        out_specs=pl.BlockSpec((TM, TN), lambda i, n: (i, n)),
        out_shape=jax.ShapeDtypeStruct((SEQ, D_MODEL), jnp.float32),
        scratch_shapes=[pltpu.VMEM((TM, D_MODEL), jnp.bfloat16)],
        compiler_params=_params("parallel", "arbitrary"),
        name="ssm_out",
    )(y, w_glu, w_glu, x)


def _mlp_kernel(x_ref, g_ref, wu_ref, wd_ref, o_ref, h_ref):
    f = pl.program_id(1)

    @pl.when(f == 0)
    def _():
        x = x_ref[...]
        h_ref[...] = _rms(x, g_ref[...]).astype(jnp.bfloat16)
        o_ref[...] = x

    a = jnp.maximum(jnp.dot(h_ref[...], wu_ref[...], preferred_element_type=jnp.float32), 0.0)
    a = (a * a).astype(jnp.bfloat16)
    o_ref[...] += jnp.dot(a, wd_ref[...], preferred_element_type=jnp.float32)


def _mlp(x, g, w_up, w_down, layer):
    return pl.pallas_call(
        _mlp_kernel,
        grid=(SEQ // TM_MLP, D_FF // TF_MLP),
        in_specs=[
            pl.BlockSpec((TM_MLP, D_MODEL), lambda i, f: (i, 0)),
            pl.BlockSpec((1, D_MODEL), lambda i, f: (0, 0)),
            pl.BlockSpec((None, D_MODEL, TF_MLP), lambda i, f: (layer, 0, f)),
            pl.BlockSpec((None, TF_MLP, D_MODEL), lambda i, f: (layer, f, 0)),
        ],
        out_specs=pl.BlockSpec((TM_MLP, D_MODEL), lambda i, f: (i, 0)),
        out_shape=jax.ShapeDtypeStruct((SEQ, D_MODEL), jnp.float32),
        scratch_shapes=[pltpu.VMEM((TM_MLP, D_MODEL), jnp.bfloat16)],
        compiler_params=_params("parallel", "arbitrary"),
        name="mlp",
    )(x, g, w_up, w_down)


def kernel(x, mix_norm, conv_w_in, conv_b_in, conv_dw, conv_dw_b, conv_ln_g, conv_ln_b, conv_w_out, conv_b_out, ssm_lambda_re, ssm_lambda_im, ssm_log_dt, ssm_b_re, ssm_b_im, ssm_c_re, ssm_c_im, ssm_d, ssm_w_glu, mlp_norm, mlp_w_up, mlp_w_down, final_norm):
    assert x.shape == (1, SEQ, D_MODEL) and x.dtype == jnp.float32
    bf16 = jnp.bfloat16
    n_blocks = SEQ // TIME_BLOCK
    x = x.reshape(n_blocks, SUBLANES, TIME_SEG, D_MODEL).transpose(0, 2, 1, 3).reshape(SEQ, D_MODEL)
    conv_w_in, conv_w_out, ssm_w_glu = conv_w_in.astype(bf16), conv_w_out.astype(bf16), ssm_w_glu.astype(bf16)
    mlp_w_up, mlp_w_down = mlp_w_up.astype(bf16), mlp_w_down.astype(bf16)
    for i in range(DEPTH):
        j = i // 2
        g = mix_norm[i].reshape(1, D_MODEL)
        if i % 2 == 0:
            v = _conv_in(x, g, conv_w_in, j, conv_b_in[j].reshape(1, 2 * D_MODEL))
            v = _dwconv(v, conv_dw[j], conv_dw_b[j].reshape(1, D_MODEL),
                        conv_ln_g[j].reshape(1, D_MODEL), conv_ln_b[j].reshape(1, D_MODEL))
            x = _conv_out(v, conv_w_out, j, conv_b_out[j].reshape(1, D_MODEL), x)
        else:
            bw, cw, ap = _ssm_tables(ssm_lambda_re[j], ssm_lambda_im[j], ssm_log_dt[j],
                                     ssm_b_re[j], ssm_b_im[j], ssm_c_re[j], ssm_c_im[j])
            h, h_bf16 = _rms_norm_pair(x, g)
            y = _ssm_core(h, h_bf16, bw, cw, ap, ssm_d[j].reshape(1, D_MODEL))
            x = _ssm_out(y, ssm_w_glu, j, x)
        x = _mlp(x, mlp_norm[i].reshape(1, D_MODEL), mlp_w_up, mlp_w_down, i)
    x = _rms_norm(x, final_norm.reshape(1, D_MODEL))
    x = x.reshape(n_blocks, TIME_SEG, SUBLANES, D_MODEL).transpose(0, 2, 1, 3)
    return x.reshape(1, SEQ, D_MODEL)
```

```python
import functools

import jax
import jax.numpy as jnp
from jax import lax
from jax.experimental import pallas as pl
from jax.experimental.pallas import tpu as pltpu

D_MODEL = 2048
SEQ = 8192
DEPTH = 4
CONV_WIDTH = 31
SSM_GROUP = 16
SSM_GROUPS = D_MODEL // SSM_GROUP
SSM_STATE = 64
D_FF = 4 * D_MODEL
EPS = 1e-6

SUBLANES = 8
LANES = 128
VMEM_LIMIT_BYTES = 56 * 1024 * 1024

TM = 1024
TN = 512
TM_MLP = 512
TF_MLP = 1024

TIME_BLOCK = 512
TIME_SEG = TIME_BLOCK // SUBLANES

CONV_HALO = 32 * SUBLANES
CONV_ROWS = 128
LN_ROWS = 16
LN_UNROLL = 4

SSM_TILE_GROUPS = 16
SSM_TILE_CH = SSM_TILE_GROUPS * SSM_GROUP
SSM_TILE_STATES = SSM_TILE_GROUPS * SSM_STATE
SSM_TILES = SSM_GROUPS // SSM_TILE_GROUPS
SSM_IN_COLS = 256


def _params(*sem):
    return pltpu.CompilerParams(dimension_semantics=sem, vmem_limit_bytes=VMEM_LIMIT_BYTES)


def _rms(x, g):
    ms = jnp.mean(x * x, axis=-1, keepdims=True)
    return x * lax.rsqrt(ms + EPS) * g


def _conv_in_kernel(x_ref, g_ref, wa_ref, wg_ref, ba_ref, bg_ref, o_ref, h_ref):
    @pl.when(pl.program_id(1) == 0)
    def _():
        h_ref[...] = _rms(x_ref[...], g_ref[...]).astype(jnp.bfloat16)

    h = h_ref[...]
    a = jnp.dot(h, wa_ref[...], preferred_element_type=jnp.float32) + ba_ref[...]
    gate = jnp.dot(h, wg_ref[...], preferred_element_type=jnp.float32) + bg_ref[...]
    v = a * jax.nn.sigmoid(gate)
    for j in range(TN // LANES):
        o_ref[j] = v[:, j * LANES:(j + 1) * LANES]


def _conv_in(x, g, w_in, b_in):
    n_col = D_MODEL // TN
    return pl.pallas_call(
        _conv_in_kernel,
        grid=(SEQ // TM, n_col),
        in_specs=[
            pl.BlockSpec((TM, D_MODEL), lambda i, n: (i, 0)),
            pl.BlockSpec((1, D_MODEL), lambda i, n: (0, 0)),
            pl.BlockSpec((D_MODEL, TN), lambda i, n: (0, n)),
            pl.BlockSpec((D_MODEL, TN), lambda i, n: (0, n + n_col)),
            pl.BlockSpec((1, TN), lambda i, n: (0, n)),
            pl.BlockSpec((1, TN), lambda i, n: (0, n + n_col)),
        ],
        out_specs=pl.BlockSpec((TN // LANES, TM, LANES), lambda i, n: (n, i, 0)),
        out_shape=jax.ShapeDtypeStruct((D_MODEL // LANES, SEQ, LANES), jnp.float32),
        scratch_shapes=[pltpu.VMEM((TM, D_MODEL), jnp.bfloat16)],
        compiler_params=_params("parallel", "arbitrary"),
        name="conv_in",
    )(x, g, w_in, w_in, b_in, b_in)


def _dwconv_kernel(v_ref, dw_ref, dwb_ref, lg_ref, lb_ref, o_ref, ext_ref, tail_ref, cv_ref):
    @pl.when(pl.program_id(0) == 0)
    def _():
        tail_ref[...] = jnp.zeros(tail_ref.shape, jnp.float32)

    tail_start = TIME_BLOCK - CONV_HALO
    first_tap = CONV_HALO - (CONV_WIDTH - 1) * SUBLANES
    first_segment = lax.broadcasted_iota(jnp.int32, (SUBLANES, LANES), 0) == 0

    def conv_lane_tile(c, carry):
        ext_ref[c, CONV_HALO:, :] = v_ref[c]
        for m in range(CONV_HALO // SUBLANES):
            rows = slice(m * SUBLANES, (m + 1) * SUBLANES)
            cur = pltpu.roll(v_ref[c, tail_start + m * SUBLANES:tail_start + (m + 1) * SUBLANES, :], 1, 0)
            prev = pltpu.roll(tail_ref[c, rows, :], 1, 0)
            ext_ref[c, rows, :] = jnp.where(first_segment, prev, cur)
        tail_ref[c] = v_ref[c, tail_start:, :]
        for r in range(TIME_BLOCK // CONV_ROWS):
            acc = jnp.zeros((CONV_ROWS, LANES), jnp.float32)
            for k in range(CONV_WIDTH):
                start = r * CONV_ROWS + first_tap + k * SUBLANES
                w = dw_ref[c, k * SUBLANES:(k + 1) * SUBLANES, :]
                acc = acc + ext_ref[c, start:start + CONV_ROWS, :] * jnp.concatenate(
                    [w] * (CONV_ROWS // SUBLANES), axis=0)
            cv_ref[c, r * CONV_ROWS:(r + 1) * CONV_ROWS, :] = acc + dwb_ref[c]
        return carry

    lax.fori_loop(0, D_MODEL // LANES, conv_lane_tile, 0)

    def norm_rows(r, carry):
        r0 = pl.multiple_of(r * LN_ROWS, LN_ROWS)
        c = cv_ref[:, pl.ds(r0, LN_ROWS), :]
        mu = jnp.sum(jnp.sum(c, axis=0), axis=-1, keepdims=True) * (1.0 / D_MODEL)
        xc = c - mu
        var = jnp.sum(jnp.sum(xc * xc, axis=0), axis=-1, keepdims=True) * (1.0 / D_MODEL)
        y = xc * lax.rsqrt(var + EPS) * lg_ref[...] + lb_ref[...]
        y = (y * jax.nn.sigmoid(y)).astype(jnp.bfloat16)
        for j in range(D_MODEL // LANES):
            o_ref[pl.ds(r0, LN_ROWS), j * LANES:(j + 1) * LANES] = y[j]
        return carry

    lax.fori_loop(0, TIME_BLOCK // LN_ROWS, norm_rows, 0, unroll=LN_UNROLL)


def _lane_tiles(p):
    p = p.reshape(-1, D_MODEL // LANES, LANES)
    return p.transpose(1, 0, 2)


def _dwconv(v, dw, dw_b, ln_g, ln_b):
    n_lt = D_MODEL // LANES
    whole = lambda i: (0, 0, 0)
    dw = _lane_tiles(jnp.repeat(dw, SUBLANES, axis=0))
    return pl.pallas_call(
        _dwconv_kernel,
        grid=(SEQ // TIME_BLOCK,),
        in_specs=[
            pl.BlockSpec((n_lt, TIME_BLOCK, LANES), lambda i: (0, i, 0)),
            pl.BlockSpec((n_lt, CONV_WIDTH * SUBLANES, LANES), whole),
            pl.BlockSpec((n_lt, 1, LANES), whole),
            pl.BlockSpec((n_lt, 1, LANES), whole),
            pl.BlockSpec((n_lt, 1, LANES), whole),
        ],
        out_specs=pl.BlockSpec((TIME_BLOCK, D_MODEL), lambda i: (i, 0)),
        out_shape=jax.ShapeDtypeStruct((SEQ, D_MODEL), jnp.bfloat16),
        scratch_shapes=[
            pltpu.VMEM((n_lt, CONV_HALO + TIME_BLOCK, LANES), jnp.float32),
            pltpu.VMEM((n_lt, CONV_HALO, LANES), jnp.float32),
            pltpu.VMEM((n_lt, TIME_BLOCK, LANES), jnp.float32),
        ],
        compiler_params=_params("arbitrary"),
        name="dwconv_ln_silu",
    )(v, dw, _lane_tiles(dw_b), _lane_tiles(ln_g), _lane_tiles(ln_b))


def _conv_out_kernel(v_ref, w_ref, b_ref, x_ref, o_ref):
    y = jnp.dot(v_ref[...], w_ref[...], preferred_element_type=jnp.float32)
    o_ref[...] = x_ref[...] + (y + b_ref[...])


def _conv_out(v, w_out, b_out, x):
    return pl.pallas_call(
        _conv_out_kernel,
        grid=(SEQ // TM, D_MODEL // TN),
        in_specs=[
            pl.BlockSpec((TM, D_MODEL), lambda i, n: (i, 0)),
            pl.BlockSpec((D_MODEL, TN), lambda i, n: (0, n)),
            pl.BlockSpec((1, TN), lambda i, n: (0, n)),
            pl.BlockSpec((TM, TN), lambda i, n: (i, n)),
        ],
        out_specs=pl.BlockSpec((TM, TN), lambda i, n: (i, n)),
        out_shape=jax.ShapeDtypeStruct((SEQ, D_MODEL), jnp.float32),
        compiler_params=_params("parallel", "parallel"),
        name="conv_out",
    )(v, w_out, b_out, x)


def _rms_pair_kernel(x_ref, g_ref, o_ref, ob_ref):
    h = _rms(x_ref[...], g_ref[...])
    o_ref[...] = h
    ob_ref[...] = h.astype(jnp.bfloat16)


def _rms_norm_pair(x, g):
    rows = pl.BlockSpec((TM, D_MODEL), lambda i: (i, 0))
    return pl.pallas_call(
        _rms_pair_kernel,
        grid=(SEQ // TM,),
        in_specs=[rows, pl.BlockSpec((1, D_MODEL), lambda i: (0, 0))],
        out_specs=[rows, rows],
        out_shape=[jax.ShapeDtypeStruct((SEQ, D_MODEL), jnp.float32),
                   jax.ShapeDtypeStruct((SEQ, D_MODEL), jnp.bfloat16)],
        compiler_params=_params("parallel"),
        name="rms_norm_pair",
    )(x, g)


def _ssm_stages(h_in_ref, h_out_ref, bw_ref, cw_ref, ap_ref, d_ref, o_ref, carry_ref, keep_carry,
                bu_w_ref, bu_r_ref, s_w_ref, s_r_ref):
    ns = SSM_TILE_STATES

    def project_in(n):
        cols = slice(n * SSM_IN_COLS, (n + 1) * SSM_IN_COLS)
        bu_w_ref[:, cols] = jnp.dot(h_in_ref[...], bw_ref[0, :, cols], preferred_element_type=jnp.float32)

    a_re = ap_ref[0, 0]
    a_im = ap_ref[0, 1]

    def advance(i, s_re, s_im):
        rows = slice(i * SUBLANES, (i + 1) * SUBLANES)
        b_re = bu_r_ref[rows, 0:ns]
        b_im = bu_r_ref[rows, ns:2 * ns]
        return (a_re * s_re - a_im * s_im + b_re, a_re * s_im + a_im * s_re + b_im)

    zero = jnp.zeros((SUBLANES, ns), jnp.float32)
    in_re = jnp.where(keep_carry, carry_ref[0], zero)
    in_im = jnp.where(keep_carry, carry_ref[1], zero)
    f_re, f_im = in_re, in_im
    n_in = 2 * ns // SSM_IN_COLS
    for i in range(TIME_SEG):
        f_re, f_im = advance(i, f_re, f_im)
        if i % (TIME_SEG // n_in) == 0:
            project_in(i // (TIME_SEG // n_in))

    for level, shift in enumerate((1, 2, 4)):
        p_re = ap_ref[0, 2 + 2 * level]
        p_im = ap_ref[0, 3 + 2 * level]
        r_re = pltpu.roll(f_re, shift, 0)
        r_im = pltpu.roll(f_im, shift, 0)
        f_re, f_im = (f_re + (p_re * r_re - p_im * r_im), f_im + (p_re * r_im + p_im * r_re))

    row = lax.broadcasted_iota(jnp.int32, (SUBLANES, ns), 0)
    e_re = pltpu.roll(f_re, 1, 0)
    e_im = pltpu.roll(f_im, 1, 0)
    carry_ref[0] = jnp.where(row == 0, e_re, zero)
    carry_ref[1] = jnp.where(row == 0, e_im, zero)
    s_re = jnp.where(row == 0, in_re, e_re)
    s_im = jnp.where(row == 0, in_im, e_im)

    y = jnp.dot(s_r_ref[...], cw_ref[0], preferred_element_type=jnp.float32)
    o_ref[...] = y + d_ref[...] * h_out_ref[...]

    for i2 in range(TIME_SEG // 2):
        m_re, m_im = advance(2 * i2, s_re, s_im)
        s_re, s_im = advance(2 * i2 + 1, m_re, m_im)
        rows = slice(i2 * 2 * SUBLANES, (i2 + 1) * 2 * SUBLANES)
        s_w_ref[rows, 0:ns] = jnp.concatenate([m_re, s_re], axis=0).astype(jnp.bfloat16)
        s_w_ref[rows, ns:2 * ns] = jnp.concatenate([m_im, s_im], axis=0).astype(jnp.bfloat16)


def _ssm_kernel(h_in_ref, h_out_ref, bw_ref, cw_ref, ap_ref, d_ref, o_ref,
                bu0_ref, bu1_ref, s0_ref, s1_ref, carry_ref):
    g = pl.program_id(0)

    @pl.when(g == 0)
    def _():
        for ref in (bu0_ref, bu1_ref, s0_ref, s1_ref, carry_ref):
            ref[...] = jnp.zeros(ref.shape, ref.dtype)

    keep_carry = g % (SEQ // TIME_BLOCK) != 1
    stages = functools.partial(_ssm_stages, h_in_ref, h_out_ref, bw_ref, cw_ref, ap_ref, d_ref, o_ref,
                               carry_ref, keep_carry)

    @pl.when(g % 2 == 0)
    def _():
        stages(bu0_ref, bu1_ref, s1_ref, s0_ref)

    @pl.when(g % 2 == 1)
    def _():
        stages(bu1_ref, bu0_ref, s0_ref, s1_ref)


def _ssm_core(h, h_bf16, bw, cw, ap, d_skip):
    ns = SSM_TILE_STATES
    n_time = SEQ // TIME_BLOCK
    n_blocks = SSM_TILES * n_time

    def block(g, lag):
        b = jnp.clip(g - lag, 0, n_blocks - 1)
        return b % n_time, b // n_time

    def h_map(lag):
        return lambda g: block(g, lag)

    def tile_map(lag, rank):
        return lambda g: (block(g, lag)[1],) + (0,) * (rank - 1)

    return pl.pallas_call(
        _ssm_kernel,
        grid=(n_blocks + 2,),
        in_specs=[
            pl.BlockSpec((TIME_BLOCK, SSM_TILE_CH), h_map(0)),
            pl.BlockSpec((TIME_BLOCK, SSM_TILE_CH), h_map(2)),
            pl.BlockSpec((1, SSM_TILE_CH, 2 * ns), tile_map(0, 3)),
            pl.BlockSpec((1, 2 * ns, SSM_TILE_CH), tile_map(2, 3)),
            pl.BlockSpec((1, 8, SUBLANES, ns), tile_map(1, 4)),
            pl.BlockSpec((1, SSM_TILE_CH), lambda g: (0, block(g, 2)[1])),
        ],
        out_specs=pl.BlockSpec((TIME_BLOCK, SSM_TILE_CH), h_map(2)),
        out_shape=jax.ShapeDtypeStruct((SEQ, D_MODEL), jnp.float32),
        scratch_shapes=[
            pltpu.VMEM((TIME_BLOCK, 2 * ns), jnp.float32),
            pltpu.VMEM((TIME_BLOCK, 2 * ns), jnp.float32),
            pltpu.VMEM((TIME_BLOCK, 2 * ns), jnp.bfloat16),
            pltpu.VMEM((TIME_BLOCK, 2 * ns), jnp.bfloat16),
            pltpu.VMEM((2, SUBLANES, ns), jnp.float32),
        ],
        compiler_params=_params("arbitrary"),
        name="ssm_scan",
    )(h_bf16, h, bw, cw, ap, d_skip)


def _ssm_tables(lam_re, lam_im, log_dt, b_re, b_im, c_re, c_im):
    f32 = jnp.float32
    dt = jnp.exp(log_dt.astype(f32))[:, None]
    lr = lam_re.astype(f32)
    li = lam_im.astype(f32)
    mag = jnp.exp(lr * dt)
    ab_re = mag * jnp.cos(li * dt)
    ab_im = mag * jnp.sin(li * dt)
    nr = ab_re - 1.0
    ni = ab_im
    den = lr * lr + li * li
    k_re = ((nr * lr + ni * li) / den)[..., None]
    k_im = ((ni * lr - nr * li) / den)[..., None]
    br = b_re.astype(f32)
    bi = b_im.astype(f32)
    bb = jnp.stack([k_re * br - k_im * bi, k_re * bi + k_im * br])
    cc = jnp.stack([c_re.astype(f32), -c_im.astype(f32)])

    tg = SSM_TILE_GROUPS
    ns2 = 2 * SSM_TILE_STATES
    bb = bb.reshape(2, SSM_TILES, tg, SSM_STATE, SSM_GROUP).transpose(1, 4, 0, 2, 3)
    cc = cc.reshape(2, SSM_TILES, tg, SSM_GROUP, SSM_STATE).transpose(1, 0, 2, 4, 3)
    bb = jnp.tile(bb.reshape(SSM_TILES, SSM_GROUP, ns2), (1, tg, 1))
    cc = jnp.tile(cc.reshape(SSM_TILES, ns2, SSM_GROUP), (1, 1, tg))
    ch_group = jnp.arange(SSM_TILE_CH) // SSM_GROUP
    col_group = (jnp.arange(ns2) % SSM_TILE_STATES) // SSM_STATE
    bw = jnp.where(ch_group[:, None] == col_group[None, :], bb, 0.0)
    cw = jnp.where(col_group[:, None] == ch_group[None, :], cc, 0.0)

    def square(z):
        return (z[0] * z[0] - z[1] * z[1], 2.0 * z[0] * z[1])

    a = (ab_re, ab_im)
    p = a
    n = 1
    while n < TIME_SEG:
        p = square(p)
        n *= 2
    assert n == TIME_SEG

    def per_sublane(z):
        z = z.reshape(SSM_TILES, 1, SSM_TILE_STATES)
        return jnp.broadcast_to(z, (SSM_TILES, SUBLANES, SSM_TILE_STATES))

    sub = jnp.arange(SUBLANES)[None, :, None]
    tables = [per_sublane(z) for z in a]
    for shift in (1, 2, 4):
        tables += [jnp.where(sub >= shift, per_sublane(z), 0.0) for z in p]
        p = square(p)
    ap = jnp.stack(tables, axis=1)
    return bw.astype(jnp.bfloat16), cw.astype(jnp.bfloat16), ap


def _ssm_out_kernel(y_ref, wv_ref, wg_ref, x_ref, o_ref, yb_ref):
    @pl.when(pl.program_id(1) == 0)
    def _():
        yb_ref[...] = jax.nn.gelu(y_ref[...]).astype(jnp.bfloat16)

    y = yb_ref[...]
    val = jnp.dot(y, wv_ref[...], preferred_element_type=jnp.float32)
    gate = jnp.dot(y, wg_ref[...], preferred_element_type=jnp.float32)
    o_ref[...] = x_ref[...] + val * jax.nn.sigmoid(gate)


def _ssm_out(y, w_glu, x):
    n_col = D_MODEL // TN
    return pl.pallas_call(
        _ssm_out_kernel,
        grid=(SEQ // TM, n_col),
        in_specs=[
            pl.BlockSpec((TM, D_MODEL), lambda i, n: (i, 0)),
            pl.BlockSpec((D_MODEL, TN), lambda i, n: (0, n)),
            pl.BlockSpec((D_MODEL, TN), lambda i, n: (0, n + n_col)),
            pl.BlockSpec((TM, TN), lambda i, n: (i, n)),
        ],
        out_specs=pl.BlockSpec((TM, TN), lambda i, n: (i, n)),
        out_shape=jax.ShapeDtypeStruct((SEQ, D_MODEL), jnp.float32),
        scratch_shapes=[pltpu.VMEM((TM, D_MODEL), jnp.bfloat16)],
        compiler_params=_params("parallel", "arbitrary"),
        name="ssm_out",
    )(y, w_glu, w_glu, x)


def _mlp_kernel(has_out_gain, n_casts, x_ref, g_ref, wu_ref, wd_ref, *rest):
    rest = list(rest)
    out_gain_ref = rest.pop(0) if has_out_gain else None
    cast_src = [rest.pop(0) for _ in range(n_casts)]
    o_ref = rest.pop(0)
    cast_dst = [rest.pop(0) for _ in range(n_casts)]
    (h_ref,) = rest
    f = pl.program_id(1)

    @pl.when(f == 0)
    def _():
        x = x_ref[...]
        h_ref[...] = _rms(x, g_ref[...]).astype(jnp.bfloat16)
        o_ref[...] = x

    a = jnp.maximum(jnp.dot(h_ref[...], wu_ref[...], preferred_element_type=jnp.float32), 0.0)
    a = (a * a).astype(jnp.bfloat16)
    o_ref[...] += jnp.dot(a, wd_ref[...], preferred_element_type=jnp.float32)

    for src_ref, dst_ref in zip(cast_src, cast_dst):
        dst_ref[...] = src_ref[...].astype(jnp.bfloat16)

    if has_out_gain:
        @pl.when(f == pl.num_programs(1) - 1)
        def _():
            o_ref[...] = _rms(o_ref[...], out_gain_ref[...])


def _mlp(x, g, w_up, w_down, out_gain=None, side_casts=()):
    n_rows, n_ff = SEQ // TM_MLP, D_FF // TF_MLP
    gain_spec = pl.BlockSpec((1, D_MODEL), lambda i, f: (0, 0))
    in_specs = [
        pl.BlockSpec((TM_MLP, D_MODEL), lambda i, f: (i, 0)),
        gain_spec,
        pl.BlockSpec((D_MODEL, TF_MLP), lambda i, f: (0, f)),
        pl.BlockSpec((TF_MLP, D_MODEL), lambda i, f: (f, 0)),
    ]
    args = [x, g, w_up, w_down]
    out_specs = [pl.BlockSpec((TM_MLP, D_MODEL), lambda i, f: (i, 0))]
    out_shape = [jax.ShapeDtypeStruct((SEQ, D_MODEL), jnp.float32)]
    if out_gain is not None:
        in_specs.append(gain_spec)
        args.append(out_gain)
    for w, layer in side_casts:
        tile = (w.shape[1] // n_rows, w.shape[2] // n_ff)
        in_specs.append(pl.BlockSpec((None,) + tile, lambda i, f, layer=layer: (layer, i, f)))
        args.append(w)
        out_specs.append(pl.BlockSpec(tile, lambda i, f: (i, f)))
        out_shape.append(jax.ShapeDtypeStruct(w.shape[1:], jnp.bfloat16))
    outs = pl.pallas_call(
        functools.partial(_mlp_kernel, out_gain is not None, len(side_casts)),
        grid=(n_rows, n_ff),
        in_specs=in_specs,
        out_specs=out_specs,
        out_shape=out_shape,
        scratch_shapes=[pltpu.VMEM((TM_MLP, D_MODEL), jnp.bfloat16)],
        compiler_params=_params("parallel", "arbitrary"),
        name="mlp",
    )(*args)
    return outs if side_casts else outs[0]


def kernel(x, mix_norm, conv_w_in, conv_b_in, conv_dw, conv_dw_b, conv_ln_g, conv_ln_b, conv_w_out, conv_b_out, ssm_lambda_re, ssm_lambda_im, ssm_log_dt, ssm_b_re, ssm_b_im, ssm_c_re, ssm_c_im, ssm_d, ssm_w_glu, mlp_norm, mlp_w_up, mlp_w_down, final_norm):
    assert x.shape == (1, SEQ, D_MODEL) and x.dtype == jnp.float32
    bf16 = jnp.bfloat16
    n_blocks = SEQ // TIME_BLOCK
    x = x.reshape(n_blocks, SUBLANES, TIME_SEG, D_MODEL).transpose(0, 2, 1, 3).reshape(SEQ, D_MODEL)
    w_mix = [conv_w_in[0].astype(bf16), conv_w_out[0].astype(bf16)]
    w_up, w_down = mlp_w_up[0].astype(bf16), mlp_w_down[0].astype(bf16)
    for i in range(DEPTH):
        j = i // 2
        g = mix_norm[i].reshape(1, D_MODEL)
        if i % 2 == 0:
            w_in, w_out = w_mix
            v = _conv_in(x, g, w_in, conv_b_in[j].reshape(1, 2 * D_MODEL))
            v = _dwconv(v, conv_dw[j], conv_dw_b[j].reshape(1, D_MODEL),
                        conv_ln_g[j].reshape(1, D_MODEL), conv_ln_b[j].reshape(1, D_MODEL))
            x = _conv_out(v, w_out, conv_b_out[j].reshape(1, D_MODEL), x)
        else:
            (w_glu,) = w_mix
            bw, cw, ap = _ssm_tables(ssm_lambda_re[j], ssm_lambda_im[j], ssm_log_dt[j],
                                     ssm_b_re[j], ssm_b_im[j], ssm_c_re[j], ssm_c_im[j])
            h, h_bf16 = _rms_norm_pair(x, g)
            y = _ssm_core(h, h_bf16, bw, cw, ap, ssm_d[j].reshape(1, D_MODEL))
            x = _ssm_out(y, w_glu, x)
        g = mlp_norm[i].reshape(1, D_MODEL)
        if i < DEPTH - 1:
            nxt = i + 1
            casts = [(mlp_w_up, nxt), (mlp_w_down, nxt)]
            if nxt % 2 == 0:
                casts += [(conv_w_in, nxt // 2), (conv_w_out, nxt // 2)]
            else:
                casts += [(ssm_w_glu, nxt // 2)]
            x, w_up, w_down, *w_mix = _mlp(x, g, w_up, w_down, side_casts=casts)
        else:
            x = _mlp(x, g, w_up, w_down, out_gain=final_norm.reshape(1, D_MODEL))
    x = x.reshape(n_blocks, TIME_SEG, SUBLANES, D_MODEL).transpose(0, 2, 1, 3)
    return x.reshape(1, SEQ, D_MODEL)
```

```python
import functools

import jax
import jax.numpy as jnp
from jax import lax
from jax.experimental import pallas as pl
from jax.experimental.pallas import tpu as pltpu

D_MODEL = 2048
SEQ = 8192
DEPTH = 4
CONV_WIDTH = 31
SSM_GROUP = 16
SSM_GROUPS = D_MODEL // SSM_GROUP
SSM_STATE = 64
D_FF = 4 * D_MODEL
EPS = 1e-6

SUBLANES = 8
LANES = 128
VMEM_LIMIT_BYTES = 56 * 1024 * 1024

TM = 1024
TN = 512
TM_MLP = 512
TF_MLP = 1024

TIME_BLOCK = 512
TIME_SEG = TIME_BLOCK // SUBLANES

CONV_HALO = 32 * SUBLANES
CONV_ROWS = 128
LN_ROWS = 16
LN_UNROLL = 4

SSM_TILE_GROUPS = 16
SSM_TILE_CH = SSM_TILE_GROUPS * SSM_GROUP
SSM_TILE_STATES = SSM_TILE_GROUPS * SSM_STATE
SSM_TILES = SSM_GROUPS // SSM_TILE_GROUPS
SSM_IN_COLS = 256


def _params(*sem):
    return pltpu.CompilerParams(dimension_semantics=sem, vmem_limit_bytes=VMEM_LIMIT_BYTES)


def _rms(x, g):
    ms = jnp.mean(x * x, axis=-1, keepdims=True)
    return x * lax.rsqrt(ms + EPS) * g


def _conv_in_kernel(x_ref, g_ref, wa_ref, wg_ref, ba_ref, bg_ref, o_ref, h_ref):
    @pl.when(pl.program_id(1) == 0)
    def _():
        h_ref[...] = _rms(x_ref[...], g_ref[...]).astype(jnp.bfloat16)

    h = h_ref[...]
    a = jnp.dot(h, wa_ref[...], preferred_element_type=jnp.float32) + ba_ref[...]
    gate = jnp.dot(h, wg_ref[...], preferred_element_type=jnp.float32) + bg_ref[...]
    v = a * jax.nn.sigmoid(gate)
    for j in range(TN // LANES):
        o_ref[j] = v[:, j * LANES:(j + 1) * LANES]


def _conv_in(x, g, w_in, b_in):
    n_col = D_MODEL // TN
    return pl.pallas_call(
        _conv_in_kernel,
        grid=(SEQ // TM, n_col),
        in_specs=[
            pl.BlockSpec((TM, D_MODEL), lambda i, n: (i, 0)),
            pl.BlockSpec((1, D_MODEL), lambda i, n: (0, 0)),
            pl.BlockSpec((D_MODEL, TN), lambda i, n: (0, n)),
            pl.BlockSpec((D_MODEL, TN), lambda i, n: (0, n + n_col)),
            pl.BlockSpec((1, TN), lambda i, n: (0, n)),
            pl.BlockSpec((1, TN), lambda i, n: (0, n + n_col)),
        ],
        out_specs=pl.BlockSpec((TN // LANES, TM, LANES), lambda i, n: (n, i, 0)),
        out_shape=jax.ShapeDtypeStruct((D_MODEL // LANES, SEQ, LANES), jnp.float32),
        scratch_shapes=[pltpu.VMEM((TM, D_MODEL), jnp.bfloat16)],
        compiler_params=_params("parallel", "arbitrary"),
        name="conv_in",
    )(x, g, w_in, w_in, b_in, b_in)


def _dwconv_kernel(v_ref, dw_ref, dwb_ref, lg_ref, lb_ref, o_ref, ext_ref, tail_ref, cv_ref):
    @pl.when(pl.program_id(0) == 0)
    def _():
        tail_ref[...] = jnp.zeros(tail_ref.shape, jnp.float32)

    tail_start = TIME_BLOCK - CONV_HALO
    first_tap = CONV_HALO - (CONV_WIDTH - 1) * SUBLANES
    first_segment = lax.broadcasted_iota(jnp.int32, (SUBLANES, LANES), 0) == 0

    def conv_lane_tile(c, carry):
        ext_ref[c, CONV_HALO:, :] = v_ref[c]
        for m in range(CONV_HALO // SUBLANES):
            rows = slice(m * SUBLANES, (m + 1) * SUBLANES)
            cur = pltpu.roll(v_ref[c, tail_start + m * SUBLANES:tail_start + (m + 1) * SUBLANES, :], 1, 0)
            prev = pltpu.roll(tail_ref[c, rows, :], 1, 0)
            ext_ref[c, rows, :] = jnp.where(first_segment, prev, cur)
        tail_ref[c] = v_ref[c, tail_start:, :]
        for r in range(TIME_BLOCK // CONV_ROWS):
            acc = jnp.zeros((CONV_ROWS, LANES), jnp.float32)
            for k in range(CONV_WIDTH):
                start = r * CONV_ROWS + first_tap + k * SUBLANES
                w = dw_ref[c, k * SUBLANES:(k + 1) * SUBLANES, :]
                acc = acc + ext_ref[c, start:start + CONV_ROWS, :] * jnp.concatenate(
                    [w] * (CONV_ROWS // SUBLANES), axis=0)
            cv_ref[c, r * CONV_ROWS:(r + 1) * CONV_ROWS, :] = acc + dwb_ref[c]
        return carry

    lax.fori_loop(0, D_MODEL // LANES, conv_lane_tile, 0)

    def norm_rows(r, carry):
        r0 = pl.multiple_of(r * LN_ROWS, LN_ROWS)
        c = cv_ref[:, pl.ds(r0, LN_ROWS), :]
        mu = jnp.sum(jnp.sum(c, axis=0), axis=-1, keepdims=True) * (1.0 / D_MODEL)
        xc = c - mu
        var = jnp.sum(jnp.sum(xc * xc, axis=0), axis=-1, keepdims=True) * (1.0 / D_MODEL)
        y = xc * lax.rsqrt(var + EPS) * lg_ref[...] + lb_ref[...]
        y = (y * jax.nn.sigmoid(y)).astype(jnp.bfloat16)
        for j in range(D_MODEL // LANES):
            o_ref[pl.ds(r0, LN_ROWS), j * LANES:(j + 1) * LANES] = y[j]
        return carry

    lax.fori_loop(0, TIME_BLOCK // LN_ROWS, norm_rows, 0, unroll=LN_UNROLL)


def _lane_tiles(p):
    p = p.reshape(-1, D_MODEL // LANES, LANES)
    return p.transpose(1, 0, 2)


def _dwconv(v, dw, dw_b, ln_g, ln_b):
    n_lt = D_MODEL // LANES
    whole = lambda i: (0, 0, 0)
    dw = _lane_tiles(jnp.repeat(dw, SUBLANES, axis=0))
    return pl.pallas_call(
        _dwconv_kernel,
        grid=(SEQ // TIME_BLOCK,),
        in_specs=[
            pl.BlockSpec((n_lt, TIME_BLOCK, LANES), lambda i: (0, i, 0)),
            pl.BlockSpec((n_lt, CONV_WIDTH * SUBLANES, LANES), whole),
            pl.BlockSpec((n_lt, 1, LANES), whole),
            pl.BlockSpec((n_lt, 1, LANES), whole),
            pl.BlockSpec((n_lt, 1, LANES), whole),
        ],
        out_specs=pl.BlockSpec((TIME_BLOCK, D_MODEL), lambda i: (i, 0)),
        out_shape=jax.ShapeDtypeStruct((SEQ, D_MODEL), jnp.bfloat16),
        scratch_shapes=[
            pltpu.VMEM((n_lt, CONV_HALO + TIME_BLOCK, LANES), jnp.float32),
            pltpu.VMEM((n_lt, CONV_HALO, LANES), jnp.float32),
            pltpu.VMEM((n_lt, TIME_BLOCK, LANES), jnp.float32),
        ],
        compiler_params=_params("arbitrary"),
        name="dwconv_ln_silu",
    )(v, dw, _lane_tiles(dw_b), _lane_tiles(ln_g), _lane_tiles(ln_b))


def _conv_out_kernel(v_ref, w_ref, b_ref, x_ref, o_ref):
    y = jnp.dot(v_ref[...], w_ref[...], preferred_element_type=jnp.float32)
    o_ref[...] = x_ref[...] + (y + b_ref[...])


def _conv_out(v, w_out, b_out, x):
    return pl.pallas_call(
        _conv_out_kernel,
        grid=(SEQ // TM, D_MODEL // TN),
        in_specs=[
            pl.BlockSpec((TM, D_MODEL), lambda i, n: (i, 0)),
            pl.BlockSpec((D_MODEL, TN), lambda i, n: (0, n)),
            pl.BlockSpec((1, TN), lambda i, n: (0, n)),
            pl.BlockSpec((TM, TN), lambda i, n: (i, n)),
        ],
        out_specs=pl.BlockSpec((TM, TN), lambda i, n: (i, n)),
        out_shape=jax.ShapeDtypeStruct((SEQ, D_MODEL), jnp.float32),
        compiler_params=_params("parallel", "parallel"),
        name="conv_out",
    )(v, w_out, b_out, x)


def _ssm_stages(h_in_ref, skip_refs, bw_ref, cw_ref, ap_ref, o_ref, carry_ref, keep_carry,
                bu_w_ref, bu_r_ref, s_w_ref, s_r_ref):
    ns = SSM_TILE_STATES
    x_ref, rstd_ref, dg_ref = skip_refs

    def project_in(n):
        cols = slice(n * SSM_IN_COLS, (n + 1) * SSM_IN_COLS)
        bu_w_ref[:, cols] = jnp.dot(h_in_ref[...], bw_ref[0, :, cols], preferred_element_type=jnp.float32)

    a_re = ap_ref[0, 0]
    a_im = ap_ref[0, 1]

    def advance(i, s_re, s_im):
        rows = slice(i * SUBLANES, (i + 1) * SUBLANES)
        b_re = bu_r_ref[rows, 0:ns]
        b_im = bu_r_ref[rows, ns:2 * ns]
        return (a_re * s_re - a_im * s_im + b_re, a_re * s_im + a_im * s_re + b_im)

    zero = jnp.zeros((SUBLANES, ns), jnp.float32)
    in_re = carry_ref[0]
    in_im = carry_ref[1]
    f_re, f_im = in_re, in_im
    n_in = 2 * ns // SSM_IN_COLS
    for i in range(TIME_SEG):
        f_re, f_im = advance(i, f_re, f_im)
        if i % (TIME_SEG // n_in) == 0:
            project_in(i // (TIME_SEG // n_in))

    for level, shift in enumerate((1, 2, 4)):
        p_re = ap_ref[0, 2 + 2 * level]
        p_im = ap_ref[0, 3 + 2 * level]
        r_re = pltpu.roll(f_re, shift, 0)
        r_im = pltpu.roll(f_im, shift, 0)
        f_re, f_im = (f_re + (p_re * r_re - p_im * r_im), f_im + (p_re * r_im + p_im * r_re))

    row = lax.broadcasted_iota(jnp.int32, (SUBLANES, ns), 0)
    e_re = pltpu.roll(f_re, 1, 0)
    e_im = pltpu.roll(f_im, 1, 0)
    hand_over = (row == 0) & keep_carry
    carry_ref[0] = jnp.where(hand_over, e_re, zero)
    carry_ref[1] = jnp.where(hand_over, e_im, zero)
    s_re = jnp.where(row == 0, in_re, e_re)
    s_im = jnp.where(row == 0, in_im, e_im)

    o_ref[...] = jnp.dot(s_r_ref[...], cw_ref[0], preferred_element_type=jnp.float32)

    for i2 in range(TIME_SEG // 2):
        m_re, m_im = advance(2 * i2, s_re, s_im)
        s_re, s_im = advance(2 * i2 + 1, m_re, m_im)
        rows = slice(i2 * 2 * SUBLANES, (i2 + 1) * 2 * SUBLANES)
        s_w_ref[rows, 0:ns] = jnp.concatenate([m_re, s_re], axis=0).astype(jnp.bfloat16)
        s_w_ref[rows, ns:2 * ns] = jnp.concatenate([m_im, s_im], axis=0).astype(jnp.bfloat16)

    rstd = jnp.concatenate([rstd_ref[...]] * (SSM_TILE_CH // LANES), axis=1)
    o_ref[...] += dg_ref[...] * (x_ref[...] * rstd)


def _ssm_kernel(h_in_ref, x_ref, bw_ref, cw_ref, ap_ref, dg_ref, rstd_ref, o_ref,
                bu0_ref, bu1_ref, s0_ref, s1_ref, carry_ref):
    g = pl.program_id(0)

    @pl.when(g == 0)
    def _():
        for ref in (bu0_ref, bu1_ref, s0_ref, s1_ref, carry_ref):
            ref[...] = jnp.zeros(ref.shape, ref.dtype)

    keep_carry = g % (SEQ // TIME_BLOCK) != 0
    stages = functools.partial(_ssm_stages, h_in_ref, (x_ref, rstd_ref, dg_ref), bw_ref, cw_ref,
                               ap_ref, o_ref, carry_ref, keep_carry)

    @pl.when(g % 2 == 0)
    def _():
        stages(bu0_ref, bu1_ref, s1_ref, s0_ref)

    @pl.when(g % 2 == 1)
    def _():
        stages(bu1_ref, bu0_ref, s0_ref, s1_ref)


def _ssm_core(h_bf16, x, rstd, d_gain, bw, cw, ap):
    ns = SSM_TILE_STATES
    n_time = SEQ // TIME_BLOCK
    n_blocks = SSM_TILES * n_time

    def block(g, lag):
        b = jnp.clip(g - lag, 0, n_blocks - 1)
        return b % n_time, b // n_time

    def h_map(lag):
        return lambda g: block(g, lag)

    def tile_map(lag, rank):
        return lambda g: (block(g, lag)[1],) + (0,) * (rank - 1)

    return pl.pallas_call(
        _ssm_kernel,
        grid=(n_blocks + 2,),
        in_specs=[
            pl.BlockSpec((TIME_BLOCK, SSM_TILE_CH), h_map(0)),
            pl.BlockSpec((TIME_BLOCK, SSM_TILE_CH), h_map(2)),
            pl.BlockSpec((1, SSM_TILE_CH, 2 * ns), tile_map(0, 3)),
            pl.BlockSpec((1, 2 * ns, SSM_TILE_CH), tile_map(2, 3)),
            pl.BlockSpec((1, 8, SUBLANES, ns), tile_map(1, 4)),
            pl.BlockSpec((1, SSM_TILE_CH), lambda g: (0, block(g, 2)[1])),
            pl.BlockSpec((TIME_BLOCK, LANES), lambda g: (block(g, 2)[0], 0)),
        ],
        out_specs=pl.BlockSpec((TIME_BLOCK, SSM_TILE_CH), h_map(2)),
        out_shape=jax.ShapeDtypeStruct((SEQ, D_MODEL), jnp.float32),
        scratch_shapes=[
            pltpu.VMEM((TIME_BLOCK, 2 * ns), jnp.float32),
            pltpu.VMEM((TIME_BLOCK, 2 * ns), jnp.float32),
            pltpu.VMEM((TIME_BLOCK, 2 * ns), jnp.bfloat16),
            pltpu.VMEM((TIME_BLOCK, 2 * ns), jnp.bfloat16),
            pltpu.VMEM((2, SUBLANES, ns), jnp.float32),
        ],
        compiler_params=_params("arbitrary"),
        name="ssm_scan",
    )(h_bf16, x, bw, cw, ap, d_gain, rstd)


def _ssm_tables(lam_re, lam_im, log_dt, b_re, b_im, c_re, c_im):
    f32 = jnp.float32
    dt = jnp.exp(log_dt.astype(f32))[:, None]
    lr = lam_re.astype(f32)
    li = lam_im.astype(f32)
    mag = jnp.exp(lr * dt)
    ab_re = mag * jnp.cos(li * dt)
    ab_im = mag * jnp.sin(li * dt)
    nr = ab_re - 1.0
    ni = ab_im
    den = lr * lr + li * li
    k_re = ((nr * lr + ni * li) / den)[..., None]
    k_im = ((ni * lr - nr * li) / den)[..., None]
    br = b_re.astype(f32)
    bi = b_im.astype(f32)
    bb = jnp.stack([k_re * br - k_im * bi, k_re * bi + k_im * br])
    cc = jnp.stack([c_re.astype(f32), -c_im.astype(f32)])

    tg = SSM_TILE_GROUPS
    ns2 = 2 * SSM_TILE_STATES
    bb = bb.reshape(2, SSM_TILES, tg, SSM_STATE, SSM_GROUP).transpose(1, 4, 0, 2, 3)
    cc = cc.reshape(2, SSM_TILES, tg, SSM_GROUP, SSM_STATE).transpose(1, 0, 2, 4, 3)
    bb = jnp.tile(bb.reshape(SSM_TILES, SSM_GROUP, ns2), (1, tg, 1))
    cc = jnp.tile(cc.reshape(SSM_TILES, ns2, SSM_GROUP), (1, 1, tg))
    ch_group = jnp.arange(SSM_TILE_CH) // SSM_GROUP
    col_group = (jnp.arange(ns2) % SSM_TILE_STATES) // SSM_STATE
    bw = jnp.where(ch_group[:, None] == col_group[None, :], bb, 0.0)
    cw = jnp.where(col_group[:, None] == ch_group[None, :], cc, 0.0)

    def square(z):
        return (z[0] * z[0] - z[1] * z[1], 2.0 * z[0] * z[1])

    a = (ab_re, ab_im)
    p = a
    n = 1
    while n < TIME_SEG:
        p = square(p)
        n *= 2
    assert n == TIME_SEG

    def per_sublane(z):
        z = z.reshape(SSM_TILES, 1, SSM_TILE_STATES)
        return jnp.broadcast_to(z, (SSM_TILES, SUBLANES, SSM_TILE_STATES))

    sub = jnp.arange(SUBLANES)[None, :, None]
    tables = [per_sublane(z) for z in a]
    for shift in (1, 2, 4):
        tables += [jnp.where(sub >= shift, per_sublane(z), 0.0) for z in p]
        p = square(p)
    ap = jnp.stack(tables, axis=1)
    return bw.astype(jnp.bfloat16), cw.astype(jnp.bfloat16), ap


def _ssm_out_kernel(y_ref, wv_ref, wg_ref, x_ref, o_ref, yb_ref):
    @pl.when(pl.program_id(1) == 0)
    def _():
        yb_ref[...] = jax.nn.gelu(y_ref[...]).astype(jnp.bfloat16)

    y = yb_ref[...]
    val = jnp.dot(y, wv_ref[...], preferred_element_type=jnp.float32)
    gate = jnp.dot(y, wg_ref[...], preferred_element_type=jnp.float32)
    o_ref[...] = x_ref[...] + val * jax.nn.sigmoid(gate)


def _ssm_out(y, w_glu, x):
    n_col = D_MODEL // TN
    return pl.pallas_call(
        _ssm_out_kernel,
        grid=(SEQ // TM, n_col),
        in_specs=[
            pl.BlockSpec((TM, D_MODEL), lambda i, n: (i, 0)),
            pl.BlockSpec((D_MODEL, TN), lambda i, n: (0, n)),
            pl.BlockSpec((D_MODEL, TN), lambda i, n: (0, n + n_col)),
            pl.BlockSpec((TM, TN), lambda i, n: (i, n)),
        ],
        out_specs=pl.BlockSpec((TM, TN), lambda i, n: (i, n)),
        out_shape=jax.ShapeDtypeStruct((SEQ, D_MODEL), jnp.float32),
        scratch_shapes=[pltpu.VMEM((TM, D_MODEL), jnp.bfloat16)],
        compiler_params=_params("parallel", "arbitrary"),
        name="ssm_out",
    )(y, w_glu, w_glu, x)


def _mlp_kernel(norm_out, n_casts, x_ref, g_ref, wu_ref, wd_ref, *rest):
    rest = list(rest)
    out_gain_ref = rest.pop(0) if norm_out else None
    cast_src = [rest.pop(0) for _ in range(n_casts)]
    o_ref = rest.pop(0)
    hb_out_ref, rstd_out_ref = (rest.pop(0), rest.pop(0)) if norm_out == "operand" else (None, None)
    cast_dst = [rest.pop(0) for _ in range(n_casts)]
    (h_ref,) = rest
    f = pl.program_id(1)

    @pl.when(f == 0)
    def _():
        x = x_ref[...]
        h_ref[...] = _rms(x, g_ref[...]).astype(jnp.bfloat16)
        o_ref[...] = x

    a = jnp.maximum(jnp.dot(h_ref[...], wu_ref[...], preferred_element_type=jnp.float32), 0.0)
    a = (a * a).astype(jnp.bfloat16)
    o_ref[...] += jnp.dot(a, wd_ref[...], preferred_element_type=jnp.float32)

    for src_ref, dst_ref in zip(cast_src, cast_dst):
        dst_ref[...] = src_ref[...].astype(jnp.bfloat16)

    if norm_out:
        @pl.when(f == pl.num_programs(1) - 1)
        def _():
            y = o_ref[...]
            rstd = lax.rsqrt(jnp.mean(y * y, axis=-1, keepdims=True) + EPS)
            normed = y * rstd * out_gain_ref[...]
            if norm_out == "final":
                o_ref[...] = normed
            else:
                hb_out_ref[...] = normed.astype(jnp.bfloat16)
                rstd_out_ref[...] = jnp.broadcast_to(rstd, rstd_out_ref.shape)


def _mlp(x, g, w_up, w_down, norm_out=None, out_gain=None, side_casts=()):
    assert (norm_out is None) == (out_gain is None)
    n_rows, n_ff = SEQ // TM_MLP, D_FF // TF_MLP
    gain_spec = pl.BlockSpec((1, D_MODEL), lambda i, f: (0, 0))
    in_specs = [
        pl.BlockSpec((TM_MLP, D_MODEL), lambda i, f: (i, 0)),
        gain_spec,
        pl.BlockSpec((D_MODEL, TF_MLP), lambda i, f: (0, f)),
        pl.BlockSpec((TF_MLP, D_MODEL), lambda i, f: (f, 0)),
    ]
    args = [x, g, w_up, w_down]
    out_specs = [pl.BlockSpec((TM_MLP, D_MODEL), lambda i, f: (i, 0))]
    out_shape = [jax.ShapeDtypeStruct((SEQ, D_MODEL), jnp.float32)]
    if norm_out:
        in_specs.append(gain_spec)
        args.append(out_gain)
    if norm_out == "operand":
        out_specs += [pl.BlockSpec((TM_MLP, D_MODEL), lambda i, f: (i, 0)),
                      pl.BlockSpec((TM_MLP, LANES), lambda i, f: (i, 0))]
        out_shape += [jax.ShapeDtypeStruct((SEQ, D_MODEL), jnp.bfloat16),
                      jax.ShapeDtypeStruct((SEQ, LANES), jnp.float32)]
    for w, layer in side_casts:
        tile = (w.shape[1] // n_rows, w.shape[2] // n_ff)
        in_specs.append(pl.BlockSpec((None,) + tile, lambda i, f, layer=layer: (layer, i, f)))
        args.append(w)
        out_specs.append(pl.BlockSpec(tile, lambda i, f: (i, f)))
        out_shape.append(jax.ShapeDtypeStruct(w.shape[1:], jnp.bfloat16))
    outs = pl.pallas_call(
        functools.partial(_mlp_kernel, norm_out, len(side_casts)),
        grid=(n_rows, n_ff),
        in_specs=in_specs,
        out_specs=out_specs,
        out_shape=out_shape,
        scratch_shapes=[pltpu.VMEM((TM_MLP, D_MODEL), jnp.bfloat16)],
        compiler_params=_params("parallel", "arbitrary"),
        name="mlp",
    )(*args)
    return outs if len(outs) > 1 else outs[0]


def kernel(x, mix_norm, conv_w_in, conv_b_in, conv_dw, conv_dw_b, conv_ln_g, conv_ln_b, conv_w_out, conv_b_out, ssm_lambda_re, ssm_lambda_im, ssm_log_dt, ssm_b_re, ssm_b_im, ssm_c_re, ssm_c_im, ssm_d, ssm_w_glu, mlp_norm, mlp_w_up, mlp_w_down, final_norm):
    assert x.shape == (1, SEQ, D_MODEL) and x.dtype == jnp.float32
    bf16 = jnp.bfloat16
    n_blocks = SEQ // TIME_BLOCK
    x = x.reshape(n_blocks, SUBLANES, TIME_SEG, D_MODEL).transpose(0, 2, 1, 3).reshape(SEQ, D_MODEL)
    w_mix = [conv_w_in[0].astype(bf16), conv_w_out[0].astype(bf16)]
    w_up, w_down = mlp_w_up[0].astype(bf16), mlp_w_down[0].astype(bf16)
    for i in range(DEPTH):
        j = i // 2
        g = mix_norm[i].reshape(1, D_MODEL)
        if i % 2 == 0:
            w_in, w_out = w_mix
            v = _conv_in(x, g, w_in, conv_b_in[j].reshape(1, 2 * D_MODEL))
            v = _dwconv(v, conv_dw[j], conv_dw_b[j].reshape(1, D_MODEL),
                        conv_ln_g[j].reshape(1, D_MODEL), conv_ln_b[j].reshape(1, D_MODEL))
            x = _conv_out(v, w_out, conv_b_out[j].reshape(1, D_MODEL), x)
        else:
            (w_glu,) = w_mix
            bw, cw, ap = _ssm_tables(ssm_lambda_re[j], ssm_lambda_im[j], ssm_log_dt[j],
                                     ssm_b_re[j], ssm_b_im[j], ssm_c_re[j], ssm_c_im[j])
            y = _ssm_core(h_bf16, x, rstd, ssm_d[j].reshape(1, D_MODEL) * g, bw, cw, ap)
            x = _ssm_out(y, w_glu, x)
        g = mlp_norm[i].reshape(1, D_MODEL)
        if i == DEPTH - 1:
            x = _mlp(x, g, w_up, w_down, norm_out="final", out_gain=final_norm.reshape(1, D_MODEL))
            continue
        nxt = i + 1
        casts = [(mlp_w_up, nxt), (mlp_w_down, nxt)]
        if nxt % 2 == 0:
            casts += [(conv_w_in, nxt // 2), (conv_w_out, nxt // 2)]
            x, w_up, w_down, *w_mix = _mlp(x, g, w_up, w_down, side_casts=casts)
        else:
            casts += [(ssm_w_glu, nxt // 2)]
            x, h_bf16, rstd, w_up, w_down, *w_mix = _mlp(
                x, g, w_up, w_down, norm_out="operand", out_gain=mix_norm[nxt].reshape(1, D_MODEL),
                side_casts=casts)
    x = x.reshape(n_blocks, TIME_SEG, SUBLANES, D_MODEL).transpose(0, 2, 1, 3)
    return x.reshape(1, SEQ, D_MODEL)
```

```python
import functools

import jax
import jax.numpy as jnp
from jax import lax
from jax.experimental import pallas as pl
from jax.experimental.pallas import tpu as pltpu

D_MODEL = 2048
SEQ = 8192
DEPTH = 4
CONV_WIDTH = 31
SSM_GROUP = 16
SSM_GROUPS = D_MODEL // SSM_GROUP
SSM_STATE = 64
D_FF = 4 * D_MODEL
EPS = 1e-6

SUBLANES = 8
LANES = 128
VMEM_LIMIT_BYTES = 56 * 1024 * 1024

TM = 1024
TN = 512
TM_MLP = 512
TF_MLP = 1024

TIME_BLOCK = 512
TIME_SEG = TIME_BLOCK // SUBLANES
TIME_PAIRS = TIME_SEG // 2

CONV_HALO = 32 * SUBLANES
CONV_ROWS = 128
LN_ROWS = 16
LN_UNROLL = 4

SSM_TILE_GROUPS = 16
SSM_TILE_CH = SSM_TILE_GROUPS * SSM_GROUP
SSM_TILE_STATES = SSM_TILE_GROUPS * SSM_STATE
SSM_TILES = SSM_GROUPS // SSM_TILE_GROUPS
SSM_IN_COLS = 256


def _params(*sem):
    return pltpu.CompilerParams(dimension_semantics=sem, vmem_limit_bytes=VMEM_LIMIT_BYTES)


def _rms(x, g):
    ms = jnp.mean(x * x, axis=-1, keepdims=True)
    return x * lax.rsqrt(ms + EPS) * g


def _conv_in_kernel(x_ref, g_ref, wa_ref, wg_ref, ba_ref, bg_ref, o_ref, h_ref):
    @pl.when(pl.program_id(1) == 0)
    def _():
        h_ref[...] = _rms(x_ref[...], g_ref[...]).astype(jnp.bfloat16)

    h = h_ref[...]
    a = jnp.dot(h, wa_ref[...], preferred_element_type=jnp.float32) + ba_ref[...]
    gate = jnp.dot(h, wg_ref[...], preferred_element_type=jnp.float32) + bg_ref[...]
    v = a * jax.nn.sigmoid(gate)
    for j in range(TN // LANES):
        o_ref[j] = v[:, j * LANES:(j + 1) * LANES]


def _conv_in(x, g, w_in, b_in):
    n_col = D_MODEL // TN
    return pl.pallas_call(
        _conv_in_kernel,
        grid=(SEQ // TM, n_col),
        in_specs=[
            pl.BlockSpec((TM, D_MODEL), lambda i, n: (i, 0)),
            pl.BlockSpec((1, D_MODEL), lambda i, n: (0, 0)),
            pl.BlockSpec((D_MODEL, TN), lambda i, n: (0, n)),
            pl.BlockSpec((D_MODEL, TN), lambda i, n: (0, n + n_col)),
            pl.BlockSpec((1, TN), lambda i, n: (0, n)),
            pl.BlockSpec((1, TN), lambda i, n: (0, n + n_col)),
        ],
        out_specs=pl.BlockSpec((TN // LANES, TM, LANES), lambda i, n: (n, i, 0)),
        out_shape=jax.ShapeDtypeStruct((D_MODEL // LANES, SEQ, LANES), jnp.float32),
        scratch_shapes=[pltpu.VMEM((TM, D_MODEL), jnp.bfloat16)],
        compiler_params=_params("parallel", "arbitrary"),
        name="conv_in",
    )(x, g, w_in, w_in, b_in, b_in)


def _dwconv_kernel(v_ref, dw_ref, dwb_ref, lg_ref, lb_ref, o_ref, ext_ref, tail_ref, cv_ref):
    @pl.when(pl.program_id(0) == 0)
    def _():
        tail_ref[...] = jnp.zeros(tail_ref.shape, jnp.float32)

    tail_start = TIME_BLOCK - CONV_HALO
    first_tap = CONV_HALO - (CONV_WIDTH - 1) * SUBLANES
    first_segment = lax.broadcasted_iota(jnp.int32, (SUBLANES, LANES), 0) == 0

    def conv_lane_tile(c, carry):
        ext_ref[c, CONV_HALO:, :] = v_ref[c]
        for m in range(CONV_HALO // SUBLANES):
            rows = slice(m * SUBLANES, (m + 1) * SUBLANES)
            cur = pltpu.roll(v_ref[c, tail_start + m * SUBLANES:tail_start + (m + 1) * SUBLANES, :], 1, 0)
            prev = pltpu.roll(tail_ref[c, rows, :], 1, 0)
            ext_ref[c, rows, :] = jnp.where(first_segment, prev, cur)
        tail_ref[c] = v_ref[c, tail_start:, :]
        for r in range(TIME_BLOCK // CONV_ROWS):
            acc = jnp.zeros((CONV_ROWS, LANES), jnp.float32)
            for k in range(CONV_WIDTH):
                start = r * CONV_ROWS + first_tap + k * SUBLANES
                w = dw_ref[c, k * SUBLANES:(k + 1) * SUBLANES, :]
                acc = acc + ext_ref[c, start:start + CONV_ROWS, :] * jnp.concatenate(
                    [w] * (CONV_ROWS // SUBLANES), axis=0)
            cv_ref[c, r * CONV_ROWS:(r + 1) * CONV_ROWS, :] = acc + dwb_ref[c]
        return carry

    lax.fori_loop(0, D_MODEL // LANES, conv_lane_tile, 0)

    def norm_rows(r, carry):
        r0 = pl.multiple_of(r * LN_ROWS, LN_ROWS)
        c = cv_ref[:, pl.ds(r0, LN_ROWS), :]
        mu = jnp.sum(jnp.sum(c, axis=0), axis=-1, keepdims=True) * (1.0 / D_MODEL)
        xc = c - mu
        var = jnp.sum(jnp.sum(xc * xc, axis=0), axis=-1, keepdims=True) * (1.0 / D_MODEL)
        y = xc * lax.rsqrt(var + EPS) * lg_ref[...] + lb_ref[...]
        y = (y * jax.nn.sigmoid(y)).astype(jnp.bfloat16)
        for j in range(D_MODEL // LANES):
            o_ref[pl.ds(r0, LN_ROWS), j * LANES:(j + 1) * LANES] = y[j]
        return carry

    lax.fori_loop(0, TIME_BLOCK // LN_ROWS, norm_rows, 0, unroll=LN_UNROLL)


def _lane_tiles(p):
    p = p.reshape(-1, D_MODEL // LANES, LANES)
    return p.transpose(1, 0, 2)


def _dwconv(v, dw, dw_b, ln_g, ln_b):
    n_lt = D_MODEL // LANES
    whole = lambda i: (0, 0, 0)
    dw = _lane_tiles(jnp.repeat(dw, SUBLANES, axis=0))
    return pl.pallas_call(
        _dwconv_kernel,
        grid=(SEQ // TIME_BLOCK,),
        in_specs=[
            pl.BlockSpec((n_lt, TIME_BLOCK, LANES), lambda i: (0, i, 0)),
            pl.BlockSpec((n_lt, CONV_WIDTH * SUBLANES, LANES), whole),
            pl.BlockSpec((n_lt, 1, LANES), whole),
            pl.BlockSpec((n_lt, 1, LANES), whole),
            pl.BlockSpec((n_lt, 1, LANES), whole),
        ],
        out_specs=pl.BlockSpec((TIME_BLOCK, D_MODEL), lambda i: (i, 0)),
        out_shape=jax.ShapeDtypeStruct((SEQ, D_MODEL), jnp.bfloat16),
        scratch_shapes=[
            pltpu.VMEM((n_lt, CONV_HALO + TIME_BLOCK, LANES), jnp.float32),
            pltpu.VMEM((n_lt, CONV_HALO, LANES), jnp.float32),
            pltpu.VMEM((n_lt, TIME_BLOCK, LANES), jnp.float32),
        ],
        compiler_params=_params("arbitrary"),
        name="dwconv_ln_silu",
    )(v, dw, _lane_tiles(dw_b), _lane_tiles(ln_g), _lane_tiles(ln_b))


def _conv_out_kernel(v_ref, w_ref, b_ref, x_ref, o_ref):
    y = jnp.dot(v_ref[...], w_ref[...], preferred_element_type=jnp.float32)
    o_ref[...] = x_ref[...] + (y + b_ref[...])


def _conv_out(v, w_out, b_out, x):
    return pl.pallas_call(
        _conv_out_kernel,
        grid=(SEQ // TM, D_MODEL // TN),
        in_specs=[
            pl.BlockSpec((TM, D_MODEL), lambda i, n: (i, 0)),
            pl.BlockSpec((D_MODEL, TN), lambda i, n: (0, n)),
            pl.BlockSpec((1, TN), lambda i, n: (0, n)),
            pl.BlockSpec((TM, TN), lambda i, n: (i, n)),
        ],
        out_specs=pl.BlockSpec((TM, TN), lambda i, n: (i, n)),
        out_shape=jax.ShapeDtypeStruct((SEQ, D_MODEL), jnp.float32),
        compiler_params=_params("parallel", "parallel"),
        name="conv_out",
    )(v, w_out, b_out, x)


def _ssm_stages(h_refs, skip_refs, w_refs, ap_ref, o_ref, carry_ref, keep_carry,
                v_w_ref, v_r_ref, s_w_refs, s_r_refs):
    ns = SSM_TILE_STATES
    he_ref, ho_ref, he_out_ref = h_refs
    x_ref, rstd_ref, dg_ref = skip_refs
    w_in_ref, w_out_ref, k0_ref = w_refs
    cur_w_ref, prev_w_ref = s_w_refs
    cur_r_ref, prev_r_ref = s_r_refs

    def project_in(n):
        cols = slice(n * SSM_IN_COLS, (n + 1) * SSM_IN_COLS)
        pairs = jnp.concatenate([he_ref[...], ho_ref[...]], axis=1)
        v_w_ref[:, cols] = jnp.dot(pairs, w_in_ref[0, :, cols], preferred_element_type=jnp.float32)

    a_re = ap_ref[0, 0]
    a_im = ap_ref[0, 1]

    def advance(n, s_re, s_im):
        rows = slice(n * SUBLANES, (n + 1) * SUBLANES)
        v_re = v_r_ref[rows, 0:ns]
        v_im = v_r_ref[rows, ns:2 * ns]
        return (a_re * s_re - a_im * s_im + v_re, a_re * s_im + a_im * s_re + v_im)

    zero = jnp.zeros((SUBLANES, ns), jnp.float32)
    in_re = carry_ref[0]
    in_im = carry_ref[1]
    f_re, f_im = in_re, in_im
    n_in = 2 * ns // SSM_IN_COLS
    for n in range(TIME_PAIRS):
        f_re, f_im = advance(n, f_re, f_im)
        if n % (TIME_PAIRS // n_in) == 0:
            project_in(n // (TIME_PAIRS // n_in))

    for level, shift in enumerate((1, 2, 4)):
        p_re = ap_ref[0, 2 + 2 * level]
        p_im = ap_ref[0, 3 + 2 * level]
        r_re = pltpu.roll(f_re, shift, 0)
        r_im = pltpu.roll(f_im, shift, 0)
        f_re, f_im = (f_re + (p_re * r_re - p_im * r_im), f_im + (p_re * r_im + p_im * r_re))

    row = lax.broadcasted_iota(jnp.int32, (SUBLANES, ns), 0)
    e_re = pltpu.roll(f_re, 1, 0)
    e_im = pltpu.roll(f_im, 1, 0)
    hand_over = (row == 0) & keep_carry
    carry_ref[0] = jnp.where(hand_over, e_re, zero)
    carry_ref[1] = jnp.where(hand_over, e_im, zero)
    s_re = jnp.where(row == 0, in_re, e_re)
    s_im = jnp.where(row == 0, in_im, e_im)

    y_odd = jnp.dot(cur_r_ref[...], w_out_ref[0, 0], preferred_element_type=jnp.float32)
    y_even = (jnp.dot(prev_r_ref[...], w_out_ref[0, 1], preferred_element_type=jnp.float32)
              + jnp.dot(he_out_ref[...], k0_ref[0], preferred_element_type=jnp.float32))
    pieces = []
    for n in range(TIME_PAIRS):
        rows = slice(n * SUBLANES, (n + 1) * SUBLANES)
        pieces += [y_even[rows], y_odd[rows]]
    o_ref[...] = jnp.concatenate(pieces, axis=0)

    for m in range(TIME_PAIRS // 2):
        p_re, p_im = s_re, s_im
        m_re, m_im = advance(2 * m, p_re, p_im)
        s_re, s_im = advance(2 * m + 1, m_re, m_im)
        rows = slice(m * 2 * SUBLANES, (m + 1) * 2 * SUBLANES)
        cur_w_ref[rows, 0:ns] = jnp.concatenate([m_re, s_re], axis=0).astype(jnp.bfloat16)
        cur_w_ref[rows, ns:2 * ns] = jnp.concatenate([m_im, s_im], axis=0).astype(jnp.bfloat16)
        prev_w_ref[rows, 0:ns] = jnp.concatenate([p_re, m_re], axis=0).astype(jnp.bfloat16)
        prev_w_ref[rows, ns:2 * ns] = jnp.concatenate([p_im, m_im], axis=0).astype(jnp.bfloat16)

    rstd = jnp.concatenate([rstd_ref[...]] * (SSM_TILE_CH // LANES), axis=1)
    o_ref[...] += dg_ref[...] * (x_ref[...] * rstd)


def _ssm_kernel(he_ref, ho_ref, he_out_ref, x_ref, w_in_ref, w_out_ref, k0_ref, ap_ref, dg_ref, rstd_ref,
                o_ref, v0_ref, v1_ref, cur0_ref, cur1_ref, prev0_ref, prev1_ref, carry_ref):
    g = pl.program_id(0)

    @pl.when(g == 0)
    def _():
        for ref in (v0_ref, v1_ref, cur0_ref, cur1_ref, prev0_ref, prev1_ref, carry_ref):
            ref[...] = jnp.zeros(ref.shape, ref.dtype)

    keep_carry = g % (SEQ // TIME_BLOCK) != 0
    stages = functools.partial(_ssm_stages, (he_ref, ho_ref, he_out_ref), (x_ref, rstd_ref, dg_ref),
                               (w_in_ref, w_out_ref, k0_ref), ap_ref, o_ref, carry_ref, keep_carry)

    @pl.when(g % 2 == 0)
    def _():
        stages(v0_ref, v1_ref, (cur1_ref, prev1_ref), (cur0_ref, prev0_ref))

    @pl.when(g % 2 == 1)
    def _():
        stages(v1_ref, v0_ref, (cur0_ref, prev0_ref), (cur1_ref, prev1_ref))


def _ssm_core(h_even, h_odd, x, rstd, d_gain, w_in, w_out, k0, ap):
    ns = SSM_TILE_STATES
    n_time = SEQ // TIME_BLOCK
    n_blocks = SSM_TILES * n_time
    pair_rows = TIME_PAIRS * SUBLANES

    def block(g, lag):
        b = jnp.clip(g - lag, 0, n_blocks - 1)
        return b % n_time, b // n_time

    def rows_map(lag):
        return lambda g: block(g, lag)

    def tile_map(lag, rank):
        return lambda g: (block(g, lag)[1],) + (0,) * (rank - 1)

    return pl.pallas_call(
        _ssm_kernel,
        grid=(n_blocks + 2,),
        in_specs=[
            pl.BlockSpec((pair_rows, SSM_TILE_CH), rows_map(0)),
            pl.BlockSpec((pair_rows, SSM_TILE_CH), rows_map(0)),
            pl.BlockSpec((pair_rows, SSM_TILE_CH), rows_map(2)),
            pl.BlockSpec((TIME_BLOCK, SSM_TILE_CH), rows_map(2)),
            pl.BlockSpec((1, 2 * SSM_TILE_CH, 2 * ns), tile_map(0, 3)),
            pl.BlockSpec((1, 2, 2 * ns, SSM_TILE_CH), tile_map(2, 4)),
            pl.BlockSpec((1, SSM_TILE_CH, SSM_TILE_CH), tile_map(2, 3)),
            pl.BlockSpec((1, 8, SUBLANES, ns), tile_map(1, 4)),
            pl.BlockSpec((1, SSM_TILE_CH), lambda g: (0, block(g, 2)[1])),
            pl.BlockSpec((TIME_BLOCK, LANES), lambda g: (block(g, 2)[0], 0)),
        ],
        out_specs=pl.BlockSpec((TIME_BLOCK, SSM_TILE_CH), rows_map(2)),
        out_shape=jax.ShapeDtypeStruct((SEQ, D_MODEL), jnp.float32),
        scratch_shapes=[pltpu.VMEM((pair_rows, 2 * ns), jnp.float32)] * 2
        + [pltpu.VMEM((pair_rows, 2 * ns), jnp.bfloat16)] * 4
        + [pltpu.VMEM((2, SUBLANES, ns), jnp.float32)],
        compiler_params=_params("arbitrary"),
        name="ssm_scan",
    )(h_even, h_odd, h_even, x, w_in, w_out, k0, ap, d_gain, rstd)


def _ssm_tables(lam_re, lam_im, log_dt, b_re, b_im, c_re, c_im):
    f32 = jnp.float32
    dt = jnp.exp(log_dt.astype(f32))[:, None]
    lr = lam_re.astype(f32)
    li = lam_im.astype(f32)
    mag = jnp.exp(lr * dt)
    ab_re = mag * jnp.cos(li * dt)
    ab_im = mag * jnp.sin(li * dt)
    nr = ab_re - 1.0
    ni = ab_im
    den = lr * lr + li * li
    k_re = ((nr * lr + ni * li) / den)[..., None]
    k_im = ((ni * lr - nr * li) / den)[..., None]
    br = b_re.astype(f32)
    bi = b_im.astype(f32)
    bb_re = k_re * br - k_im * bi
    bb_im = k_re * bi + k_im * br
    cr = c_re.astype(f32)
    ci = c_im.astype(f32)

    tg = SSM_TILE_GROUPS
    ns2 = 2 * SSM_TILE_STATES
    ch_group = jnp.arange(SSM_TILE_CH) // SSM_GROUP
    col_group = (jnp.arange(ns2) % SSM_TILE_STATES) // SSM_STATE

    def tile_in(w_re, w_im):
        w = jnp.stack([w_re, w_im]).reshape(2, SSM_TILES, tg, SSM_STATE, SSM_GROUP)
        w = w.transpose(1, 4, 0, 2, 3).reshape(SSM_TILES, SSM_GROUP, ns2)
        w = jnp.tile(w, (1, tg, 1))
        return jnp.where(ch_group[:, None] == col_group[None, :], w, 0.0)

    def tile_out(w_re, w_im):
        w = jnp.stack([w_re, -w_im]).reshape(2, SSM_TILES, tg, SSM_GROUP, SSM_STATE)
        w = w.transpose(1, 0, 2, 4, 3).reshape(SSM_TILES, ns2, SSM_GROUP)
        w = jnp.tile(w, (1, 1, tg))
        return jnp.where(col_group[:, None] == ch_group[None, :], w, 0.0)

    a_re_b, a_im_b = ab_re[..., None], ab_im[..., None]
    w_in = jnp.concatenate([tile_in(a_re_b * bb_re - a_im_b * bb_im, a_re_b * bb_im + a_im_b * bb_re),
                            tile_in(bb_re, bb_im)], axis=1)
    a_re_c, a_im_c = ab_re[:, None, :], ab_im[:, None, :]
    w_out = jnp.stack([tile_out(cr, ci),
                       tile_out(cr * a_re_c - ci * a_im_c, cr * a_im_c + ci * a_re_c)], axis=1)
    direct = (jnp.einsum("gcp,gpd->gdc", cr, bb_re, precision="highest")
              - jnp.einsum("gcp,gpd->gdc", ci, bb_im, precision="highest"))
    direct = direct.reshape(SSM_TILES, tg, SSM_GROUP, SSM_GROUP)
    k0 = jnp.einsum("ab,qadc->qadbc", jnp.eye(tg, dtype=f32), direct).reshape(
        SSM_TILES, SSM_TILE_CH, SSM_TILE_CH)

    def square(z):
        return (z[0] * z[0] - z[1] * z[1], 2.0 * z[0] * z[1])

    a2 = square((ab_re, ab_im))
    p = a2
    n = 2
    while n < TIME_SEG:
        p = square(p)
        n *= 2
    assert n == TIME_SEG

    def per_sublane(z):
        z = z.reshape(SSM_TILES, 1, SSM_TILE_STATES)
        return jnp.broadcast_to(z, (SSM_TILES, SUBLANES, SSM_TILE_STATES))

    sub = jnp.arange(SUBLANES)[None, :, None]
    tables = [per_sublane(z) for z in a2]
    for shift in (1, 2, 4):
        tables += [jnp.where(sub >= shift, per_sublane(z), 0.0) for z in p]
        p = square(p)
    ap = jnp.stack(tables, axis=1)
    bf16 = jnp.bfloat16
    return w_in.astype(bf16), w_out.astype(bf16), k0.astype(bf16), ap


def _ssm_out_kernel(y_ref, wv_ref, wg_ref, x_ref, o_ref, yb_ref):
    @pl.when(pl.program_id(1) == 0)
    def _():
        yb_ref[...] = jax.nn.gelu(y_ref[...]).astype(jnp.bfloat16)

    y = yb_ref[...]
    val = jnp.dot(y, wv_ref[...], preferred_element_type=jnp.float32)
    gate = jnp.dot(y, wg_ref[...], preferred_element_type=jnp.float32)
    o_ref[...] = x_ref[...] + val * jax.nn.sigmoid(gate)


def _ssm_out(y, w_glu, x):
    n_col = D_MODEL // TN
    return pl.pallas_call(
        _ssm_out_kernel,
        grid=(SEQ // TM, n_col),
        in_specs=[
            pl.BlockSpec((TM, D_MODEL), lambda i, n: (i, 0)),
            pl.BlockSpec((D_MODEL, TN), lambda i, n: (0, n)),
            pl.BlockSpec((D_MODEL, TN), lambda i, n: (0, n + n_col)),
            pl.BlockSpec((TM, TN), lambda i, n: (i, n)),
        ],
        out_specs=pl.BlockSpec((TM, TN), lambda i, n: (i, n)),
        out_shape=jax.ShapeDtypeStruct((SEQ, D_MODEL), jnp.float32),
        scratch_shapes=[pltpu.VMEM((TM, D_MODEL), jnp.bfloat16)],
        compiler_params=_params("parallel", "arbitrary"),
        name="ssm_out",
    )(y, w_glu, w_glu, x)


def _mlp_kernel(norm_out, n_casts, x_ref, g_ref, wu_ref, wd_ref, *rest):
    rest = list(rest)
    out_gain_ref = rest.pop(0) if norm_out else None
    cast_src = [rest.pop(0) for _ in range(n_casts)]
    o_ref = rest.pop(0)
    operand_refs = [rest.pop(0) for _ in range(3)] if norm_out == "operand" else None
    cast_dst = [rest.pop(0) for _ in range(n_casts)]
    (h_ref,) = rest
    f = pl.program_id(1)

    @pl.when(f == 0)
    def _():
        x = x_ref[...]
        h_ref[...] = _rms(x, g_ref[...]).astype(jnp.bfloat16)
        o_ref[...] = x

    a = jnp.maximum(jnp.dot(h_ref[...], wu_ref[...], preferred_element_type=jnp.float32), 0.0)
    a = (a * a).astype(jnp.bfloat16)
    o_ref[...] += jnp.dot(a, wd_ref[...], preferred_element_type=jnp.float32)

    for src_ref, dst_ref in zip(cast_src, cast_dst):
        dst_ref[...] = src_ref[...].astype(jnp.bfloat16)

    if norm_out:
        @pl.when(f == pl.num_programs(1) - 1)
        def _():
            y = o_ref[...]
            rstd = lax.rsqrt(jnp.mean(y * y, axis=-1, keepdims=True) + EPS)
            normed = y * rstd * out_gain_ref[...]
            if norm_out == "final":
                o_ref[...] = normed
            else:
                even_ref, odd_ref, rstd_out_ref = operand_refs
                groups = [normed[r * SUBLANES:(r + 1) * SUBLANES] for r in range(TM_MLP // SUBLANES)]
                even_ref[...] = jnp.concatenate(groups[0::2], axis=0).astype(jnp.bfloat16)
                odd_ref[...] = jnp.concatenate(groups[1::2], axis=0).astype(jnp.bfloat16)
                rstd_out_ref[...] = jnp.broadcast_to(rstd, rstd_out_ref.shape)


def _mlp(x, g, w_up, w_down, norm_out=None, out_gain=None, side_casts=()):
    assert (norm_out is None) == (out_gain is None)
    n_rows, n_ff = SEQ // TM_MLP, D_FF // TF_MLP
    gain_spec = pl.BlockSpec((1, D_MODEL), lambda i, f: (0, 0))
    in_specs = [
        pl.BlockSpec((TM_MLP, D_MODEL), lambda i, f: (i, 0)),
        gain_spec,
        pl.BlockSpec((D_MODEL, TF_MLP), lambda i, f: (0, f)),
        pl.BlockSpec((TF_MLP, D_MODEL), lambda i, f: (f, 0)),
    ]
    args = [x, g, w_up, w_down]
    out_specs = [pl.BlockSpec((TM_MLP, D_MODEL), lambda i, f: (i, 0))]
    out_shape = [jax.ShapeDtypeStruct((SEQ, D_MODEL), jnp.float32)]
    if norm_out:
        in_specs.append(gain_spec)
        args.append(out_gain)
    if norm_out == "operand":
        half = pl.BlockSpec((TM_MLP // 2, D_MODEL), lambda i, f: (i, 0))
        out_specs += [half, half, pl.BlockSpec((TM_MLP, LANES), lambda i, f: (i, 0))]
        out_shape += [jax.ShapeDtypeStruct((SEQ // 2, D_MODEL), jnp.bfloat16)] * 2
        out_shape += [jax.ShapeDtypeStruct((SEQ, LANES), jnp.float32)]
    for w, layer in side_casts:
        tile = (w.shape[1] // n_rows, w.shape[2] // n_ff)
        in_specs.append(pl.BlockSpec((None,) + tile, lambda i, f, layer=layer: (layer, i, f)))
        args.append(w)
        out_specs.append(pl.BlockSpec(tile, lambda i, f: (i, f)))
        out_shape.append(jax.ShapeDtypeStruct(w.shape[1:], jnp.bfloat16))
    outs = pl.pallas_call(
        functools.partial(_mlp_kernel, norm_out, len(side_casts)),
        grid=(n_rows, n_ff),
        in_specs=in_specs,
        out_specs=out_specs,
        out_shape=out_shape,
        scratch_shapes=[pltpu.VMEM((TM_MLP, D_MODEL), jnp.bfloat16)],
        compiler_params=_params("parallel", "arbitrary"),
        name="mlp",
    )(*args)
    return outs if len(outs) > 1 else outs[0]


def kernel(x, mix_norm, conv_w_in, conv_b_in, conv_dw, conv_dw_b, conv_ln_g, conv_ln_b, conv_w_out, conv_b_out, ssm_lambda_re, ssm_lambda_im, ssm_log_dt, ssm_b_re, ssm_b_im, ssm_c_re, ssm_c_im, ssm_d, ssm_w_glu, mlp_norm, mlp_w_up, mlp_w_down, final_norm):
    assert x.shape == (1, SEQ, D_MODEL) and x.dtype == jnp.float32
    bf16 = jnp.bfloat16
    n_blocks = SEQ // TIME_BLOCK
    x = x.reshape(n_blocks, SUBLANES, TIME_SEG, D_MODEL).transpose(0, 2, 1, 3).reshape(SEQ, D_MODEL)
    w_mix = [conv_w_in[0].astype(bf16), conv_w_out[0].astype(bf16)]
    w_up, w_down = mlp_w_up[0].astype(bf16), mlp_w_down[0].astype(bf16)
    for i in range(DEPTH):
        j = i // 2
        g = mix_norm[i].reshape(1, D_MODEL)
        if i % 2 == 0:
            w_in, w_out = w_mix
            v = _conv_in(x, g, w_in, conv_b_in[j].reshape(1, 2 * D_MODEL))
            v = _dwconv(v, conv_dw[j], conv_dw_b[j].reshape(1, D_MODEL),
                        conv_ln_g[j].reshape(1, D_MODEL), conv_ln_b[j].reshape(1, D_MODEL))
            x = _conv_out(v, w_out, conv_b_out[j].reshape(1, D_MODEL), x)
        else:
            (w_glu,) = w_mix
            tables = _ssm_tables(ssm_lambda_re[j], ssm_lambda_im[j], ssm_log_dt[j],
                                 ssm_b_re[j], ssm_b_im[j], ssm_c_re[j], ssm_c_im[j])
            y = _ssm_core(h_even, h_odd, x, rstd, ssm_d[j].reshape(1, D_MODEL) * g, *tables)
            x = _ssm_out(y, w_glu, x)
        g = mlp_norm[i].reshape(1, D_MODEL)
        if i == DEPTH - 1:
            x = _mlp(x, g, w_up, w_down, norm_out="final", out_gain=final_norm.reshape(1, D_MODEL))
            continue
        nxt = i + 1
        casts = [(mlp_w_up, nxt), (mlp_w_down, nxt)]
        if nxt % 2 == 0:
            casts += [(conv_w_in, nxt // 2), (conv_w_out, nxt // 2)]
            x, w_up, w_down, *w_mix = _mlp(x, g, w_up, w_down, side_casts=casts)
        else:
            casts += [(ssm_w_glu, nxt // 2)]
            x, h_even, h_odd, rstd, w_up, w_down, *w_mix = _mlp(
                x, g, w_up, w_down, norm_out="operand", out_gain=mix_norm[nxt].reshape(1, D_MODEL),
                side_casts=casts)
    x = x.reshape(n_blocks, TIME_SEG, SUBLANES, D_MODEL).transpose(0, 2, 1, 3)
    return x.reshape(1, SEQ, D_MODEL)
```

```python
import functools

import jax
import jax.numpy as jnp
from jax import lax
from jax.experimental import pallas as pl
from jax.experimental.pallas import tpu as pltpu

D_MODEL = 2048
SEQ = 8192
DEPTH = 4
CONV_WIDTH = 31
SSM_GROUP = 16
SSM_GROUPS = D_MODEL // SSM_GROUP
SSM_STATE = 64
D_FF = 4 * D_MODEL
EPS = 1e-6

SUBLANES = 8
LANES = 128
VMEM_LIMIT_BYTES = 56 * 1024 * 1024

TM = 1024
TN = 512
TM_MLP = 512
TF_MLP = 1024

TIME_BLOCK = 512
TIME_SEG = TIME_BLOCK // SUBLANES
TIME_PAIRS = TIME_SEG // 2

CONV_HALO = 32 * SUBLANES
CONV_ROWS = 128
LN_ROWS = 16
LN_UNROLL = 4

SSM_TILE_GROUPS = 16
SSM_TILE_CH = SSM_TILE_GROUPS * SSM_GROUP
SSM_TILE_STATES = SSM_TILE_GROUPS * SSM_STATE
SSM_TILES = SSM_GROUPS // SSM_TILE_GROUPS
SSM_IN_COLS = 256


def _params(*sem):
    return pltpu.CompilerParams(dimension_semantics=sem, vmem_limit_bytes=VMEM_LIMIT_BYTES)


def _rms(x, g):
    ms = jnp.mean(x * x, axis=-1, keepdims=True)
    return x * lax.rsqrt(ms + EPS) * g


def _conv_in_kernel(x_ref, g_ref, wa_ref, wg_ref, ba_ref, bg_ref, o_ref, h_ref):
    @pl.when(pl.program_id(1) == 0)
    def _():
        h_ref[...] = _rms(x_ref[...], g_ref[...]).astype(jnp.bfloat16)

    h = h_ref[...]
    a = jnp.dot(h, wa_ref[...], preferred_element_type=jnp.float32) + ba_ref[...]
    gate = jnp.dot(h, wg_ref[...], preferred_element_type=jnp.float32) + bg_ref[...]
    v = a * jax.nn.sigmoid(gate)
    for j in range(TN // LANES):
        o_ref[j] = v[:, j * LANES:(j + 1) * LANES]


def _conv_in(x, g, w_in, b_in):
    n_col = D_MODEL // TN
    return pl.pallas_call(
        _conv_in_kernel,
        grid=(SEQ // TM, n_col),
        in_specs=[
            pl.BlockSpec((TM, D_MODEL), lambda i, n: (i, 0)),
            pl.BlockSpec((1, D_MODEL), lambda i, n: (0, 0)),
            pl.BlockSpec((D_MODEL, TN), lambda i, n: (0, n)),
            pl.BlockSpec((D_MODEL, TN), lambda i, n: (0, n + n_col)),
            pl.BlockSpec((1, TN), lambda i, n: (0, n)),
            pl.BlockSpec((1, TN), lambda i, n: (0, n + n_col)),
        ],
        out_specs=pl.BlockSpec((TN // LANES, TM, LANES), lambda i, n: (n, i, 0)),
        out_shape=jax.ShapeDtypeStruct((D_MODEL // LANES, SEQ, LANES), jnp.float32),
        scratch_shapes=[pltpu.VMEM((TM, D_MODEL), jnp.bfloat16)],
        compiler_params=_params("parallel", "arbitrary"),
        name="conv_in",
    )(x, g, w_in, w_in, b_in, b_in)


def _dwconv_kernel(v_ref, dw_ref, dwb_ref, lg_ref, lb_ref, o_ref, ext_ref, tail_ref, cv_ref):
    @pl.when(pl.program_id(0) == 0)
    def _():
        tail_ref[...] = jnp.zeros(tail_ref.shape, jnp.float32)

    tail_start = TIME_BLOCK - CONV_HALO
    first_tap = CONV_HALO - (CONV_WIDTH - 1) * SUBLANES
    first_segment = lax.broadcasted_iota(jnp.int32, (SUBLANES, LANES), 0) == 0

    def conv_lane_tile(c, carry):
        ext_ref[c, CONV_HALO:, :] = v_ref[c]
        for m in range(CONV_HALO // SUBLANES):
            rows = slice(m * SUBLANES, (m + 1) * SUBLANES)
            cur = pltpu.roll(v_ref[c, tail_start + m * SUBLANES:tail_start + (m + 1) * SUBLANES, :], 1, 0)
            prev = pltpu.roll(tail_ref[c, rows, :], 1, 0)
            ext_ref[c, rows, :] = jnp.where(first_segment, prev, cur)
        tail_ref[c] = v_ref[c, tail_start:, :]
        for r in range(TIME_BLOCK // CONV_ROWS):
            acc = jnp.zeros((CONV_ROWS, LANES), jnp.float32)
            for k in range(CONV_WIDTH):
                start = r * CONV_ROWS + first_tap + k * SUBLANES
                w = dw_ref[c, k * SUBLANES:(k + 1) * SUBLANES, :]
                acc = acc + ext_ref[c, start:start + CONV_ROWS, :] * jnp.concatenate(
                    [w] * (CONV_ROWS // SUBLANES), axis=0)
            cv_ref[c, r * CONV_ROWS:(r + 1) * CONV_ROWS, :] = acc + dwb_ref[c]
        return carry

    lax.fori_loop(0, D_MODEL // LANES, conv_lane_tile, 0)

    def norm_rows(r, carry):
        r0 = pl.multiple_of(r * LN_ROWS, LN_ROWS)
        c = cv_ref[:, pl.ds(r0, LN_ROWS), :]
        mu = jnp.sum(jnp.sum(c, axis=0), axis=-1, keepdims=True) * (1.0 / D_MODEL)
        xc = c - mu
        var = jnp.sum(jnp.sum(xc * xc, axis=0), axis=-1, keepdims=True) * (1.0 / D_MODEL)
        y = xc * lax.rsqrt(var + EPS) * lg_ref[...] + lb_ref[...]
        y = (y * jax.nn.sigmoid(y)).astype(jnp.bfloat16)
        for j in range(D_MODEL // LANES):
            o_ref[pl.ds(r0, LN_ROWS), j * LANES:(j + 1) * LANES] = y[j]
        return carry

    lax.fori_loop(0, TIME_BLOCK // LN_ROWS, norm_rows, 0, unroll=LN_UNROLL)


def _lane_tiles(p):
    p = p.reshape(-1, D_MODEL // LANES, LANES)
    return p.transpose(1, 0, 2)


def _dwconv(v, dw, dw_b, ln_g, ln_b):
    n_lt = D_MODEL // LANES
    whole = lambda i: (0, 0, 0)
    dw = _lane_tiles(jnp.repeat(dw, SUBLANES, axis=0))
    return pl.pallas_call(
        _dwconv_kernel,
        grid=(SEQ // TIME_BLOCK,),
        in_specs=[
            pl.BlockSpec((n_lt, TIME_BLOCK, LANES), lambda i: (0, i, 0)),
            pl.BlockSpec((n_lt, CONV_WIDTH * SUBLANES, LANES), whole),
            pl.BlockSpec((n_lt, 1, LANES), whole),
            pl.BlockSpec((n_lt, 1, LANES), whole),
            pl.BlockSpec((n_lt, 1, LANES), whole),
        ],
        out_specs=pl.BlockSpec((TIME_BLOCK, D_MODEL), lambda i: (i, 0)),
        out_shape=jax.ShapeDtypeStruct((SEQ, D_MODEL), jnp.bfloat16),
        scratch_shapes=[
            pltpu.VMEM((n_lt, CONV_HALO + TIME_BLOCK, LANES), jnp.float32),
            pltpu.VMEM((n_lt, CONV_HALO, LANES), jnp.float32),
            pltpu.VMEM((n_lt, TIME_BLOCK, LANES), jnp.float32),
        ],
        compiler_params=_params("arbitrary"),
        name="dwconv_ln_silu",
    )(v, dw, _lane_tiles(dw_b), _lane_tiles(ln_g), _lane_tiles(ln_b))


def _conv_out_kernel(v_ref, w_ref, b_ref, x_ref, o_ref):
    y = jnp.dot(v_ref[...], w_ref[...], preferred_element_type=jnp.float32)
    o_ref[...] = x_ref[...] + (y + b_ref[...])


def _conv_out(v, w_out, b_out, x):
    return pl.pallas_call(
        _conv_out_kernel,
        grid=(SEQ // TM, D_MODEL // TN),
        in_specs=[
            pl.BlockSpec((TM, D_MODEL), lambda i, n: (i, 0)),
            pl.BlockSpec((D_MODEL, TN), lambda i, n: (0, n)),
            pl.BlockSpec((1, TN), lambda i, n: (0, n)),
            pl.BlockSpec((TM, TN), lambda i, n: (i, n)),
        ],
        out_specs=pl.BlockSpec((TM, TN), lambda i, n: (i, n)),
        out_shape=jax.ShapeDtypeStruct((SEQ, D_MODEL), jnp.float32),
        compiler_params=_params("parallel", "parallel"),
        name="conv_out",
    )(v, w_out, b_out, x)


def _ssm_stages(h_refs, skip_refs, w_refs, ap_ref, o_ref, carry_ref, keep_carry,
                v_w_ref, v_r_ref, s_w_refs, s_r_refs):
    ns = SSM_TILE_STATES
    he_ref, ho_ref, he_out_ref = h_refs
    x_ref, rstd_ref, dg_ref = skip_refs
    w_in_ref, w_out_ref, k0_ref = w_refs
    cur_w_ref, prev_w_ref = s_w_refs
    cur_r_ref, prev_r_ref = s_r_refs

    def project_in(n):
        cols = slice(n * SSM_IN_COLS, (n + 1) * SSM_IN_COLS)
        pairs = jnp.concatenate([he_ref[...], ho_ref[...]], axis=1)
        v_w_ref[:, cols] = jnp.dot(pairs, w_in_ref[0, :, cols], preferred_element_type=jnp.float32)

    a_re = ap_ref[0, 0]
    a_im = ap_ref[0, 1]

    def advance(n, s_re, s_im):
        rows = slice(n * SUBLANES, (n + 1) * SUBLANES)
        v_re = v_r_ref[rows, 0:ns]
        v_im = v_r_ref[rows, ns:2 * ns]
        return (a_re * s_re - a_im * s_im + v_re, a_re * s_im + a_im * s_re + v_im)

    zero = jnp.zeros((SUBLANES, ns), jnp.float32)
    in_re = carry_ref[0]
    in_im = carry_ref[1]
    f_re, f_im = in_re, in_im
    n_in = 2 * ns // SSM_IN_COLS
    for n in range(TIME_PAIRS):
        f_re, f_im = advance(n, f_re, f_im)
        if n % (TIME_PAIRS // n_in) == 0:
            project_in(n // (TIME_PAIRS // n_in))

    for level, shift in enumerate((1, 2, 4)):
        p_re = ap_ref[0, 2 + 2 * level]
        p_im = ap_ref[0, 3 + 2 * level]
        r_re = pltpu.roll(f_re, shift, 0)
        r_im = pltpu.roll(f_im, shift, 0)
        f_re, f_im = (f_re + (p_re * r_re - p_im * r_im), f_im + (p_re * r_im + p_im * r_re))

    row = lax.broadcasted_iota(jnp.int32, (SUBLANES, ns), 0)
    e_re = pltpu.roll(f_re, 1, 0)
    e_im = pltpu.roll(f_im, 1, 0)
    hand_over = (row == 0) & keep_carry
    carry_ref[0] = jnp.where(hand_over, e_re, zero)
    carry_ref[1] = jnp.where(hand_over, e_im, zero)
    s_re = jnp.where(row == 0, in_re, e_re)
    s_im = jnp.where(row == 0, in_im, e_im)

    y_odd = jnp.dot(cur_r_ref[...], w_out_ref[0, 0], preferred_element_type=jnp.float32)
    y_even = (jnp.dot(prev_r_ref[...], w_out_ref[0, 1], preferred_element_type=jnp.float32)
              + jnp.dot(he_out_ref[...], k0_ref[0], preferred_element_type=jnp.float32))
    pieces = []
    for n in range(TIME_PAIRS):
        rows = slice(n * SUBLANES, (n + 1) * SUBLANES)
        pieces += [y_even[rows], y_odd[rows]]
    o_ref[...] = jnp.concatenate(pieces, axis=0)

    for m in range(TIME_PAIRS // 2):
        p_re, p_im = s_re, s_im
        m_re, m_im = advance(2 * m, p_re, p_im)
        s_re, s_im = advance(2 * m + 1, m_re, m_im)
        rows = slice(m * 2 * SUBLANES, (m + 1) * 2 * SUBLANES)
        cur_w_ref[rows, 0:ns] = jnp.concatenate([m_re, s_re], axis=0).astype(jnp.bfloat16)
        cur_w_ref[rows, ns:2 * ns] = jnp.concatenate([m_im, s_im], axis=0).astype(jnp.bfloat16)
        prev_w_ref[rows, 0:ns] = jnp.concatenate([p_re, m_re], axis=0).astype(jnp.bfloat16)
        prev_w_ref[rows, ns:2 * ns] = jnp.concatenate([p_im, m_im], axis=0).astype(jnp.bfloat16)

    rstd = jnp.concatenate([rstd_ref[...]] * (SSM_TILE_CH // LANES), axis=1)
    o_ref[...] += dg_ref[...] * (x_ref[...] * rstd)


def _ssm_kernel(he_ref, ho_ref, he_out_ref, x_ref, w_in_ref, w_out_ref, k0_ref, ap_ref, dg_ref, rstd_ref,
                o_ref, v0_ref, v1_ref, cur0_ref, cur1_ref, prev0_ref, prev1_ref, carry_ref):
    g = pl.program_id(0)

    @pl.when(g == 0)
    def _():
        for ref in (v0_ref, v1_ref, cur0_ref, cur1_ref, prev0_ref, prev1_ref, carry_ref):
            ref[...] = jnp.zeros(ref.shape, ref.dtype)

    keep_carry = g % (SEQ // TIME_BLOCK) != 0
    stages = functools.partial(_ssm_stages, (he_ref, ho_ref, he_out_ref), (x_ref, rstd_ref, dg_ref),
                               (w_in_ref, w_out_ref, k0_ref), ap_ref, o_ref, carry_ref, keep_carry)

    @pl.when(g % 2 == 0)
    def _():
        stages(v0_ref, v1_ref, (cur1_ref, prev1_ref), (cur0_ref, prev0_ref))

    @pl.when(g % 2 == 1)
    def _():
        stages(v1_ref, v0_ref, (cur0_ref, prev0_ref), (cur1_ref, prev1_ref))


def _ssm_core(h_even, h_odd, x, rstd, d_gain, w_in, w_out, k0, ap):
    ns = SSM_TILE_STATES
    n_time = SEQ // TIME_BLOCK
    n_blocks = SSM_TILES * n_time
    pair_rows = TIME_PAIRS * SUBLANES

    def block(g, lag):
        b = jnp.clip(g - lag, 0, n_blocks - 1)
        return b % n_time, b // n_time

    def rows_map(lag):
        return lambda g: block(g, lag)

    def tile_map(lag, rank):
        return lambda g: (block(g, lag)[1],) + (0,) * (rank - 1)

    return pl.pallas_call(
        _ssm_kernel,
        grid=(n_blocks + 2,),
        in_specs=[
            pl.BlockSpec((pair_rows, SSM_TILE_CH), rows_map(0)),
            pl.BlockSpec((pair_rows, SSM_TILE_CH), rows_map(0)),
            pl.BlockSpec((pair_rows, SSM_TILE_CH), rows_map(2)),
            pl.BlockSpec((TIME_BLOCK, SSM_TILE_CH), rows_map(2)),
            pl.BlockSpec((1, 2 * SSM_TILE_CH, 2 * ns), tile_map(0, 3)),
            pl.BlockSpec((1, 2, 2 * ns, SSM_TILE_CH), tile_map(2, 4)),
            pl.BlockSpec((1, SSM_TILE_CH, SSM_TILE_CH), tile_map(2, 3)),
            pl.BlockSpec((1, 8, SUBLANES, ns), tile_map(1, 4)),
            pl.BlockSpec((1, SSM_TILE_CH), lambda g: (0, block(g, 2)[1])),
            pl.BlockSpec((TIME_BLOCK, LANES), lambda g: (block(g, 2)[0], 0)),
        ],
        out_specs=pl.BlockSpec((TIME_BLOCK, SSM_TILE_CH), rows_map(2)),
        out_shape=jax.ShapeDtypeStruct((SEQ, D_MODEL), jnp.float32),
        scratch_shapes=[pltpu.VMEM((pair_rows, 2 * ns), jnp.float32)] * 2
        + [pltpu.VMEM((pair_rows, 2 * ns), jnp.bfloat16)] * 4
        + [pltpu.VMEM((2, SUBLANES, ns), jnp.float32)],
        compiler_params=_params("arbitrary"),
        name="ssm_scan",
    )(h_even, h_odd, h_even, x, w_in, w_out, k0, ap, d_gain, rstd)


def _ssm_tables(lam_re, lam_im, log_dt, b_re, b_im, c_re, c_im):
    f32 = jnp.float32
    dt = jnp.exp(log_dt.astype(f32))[:, None]
    lr = lam_re.astype(f32)
    li = lam_im.astype(f32)
    mag = jnp.exp(lr * dt)
    ab_re = mag * jnp.cos(li * dt)
    ab_im = mag * jnp.sin(li * dt)
    nr = ab_re - 1.0
    ni = ab_im
    den = lr * lr + li * li
    k_re = ((nr * lr + ni * li) / den)[..., None]
    k_im = ((ni * lr - nr * li) / den)[..., None]
    br = b_re.astype(f32)
    bi = b_im.astype(f32)
    bb_re = k_re * br - k_im * bi
    bb_im = k_re * bi + k_im * br
    cr = c_re.astype(f32)
    ci = c_im.astype(f32)

    tg = SSM_TILE_GROUPS
    ns2 = 2 * SSM_TILE_STATES
    ch_group = jnp.arange(SSM_TILE_CH) // SSM_GROUP
    col_group = (jnp.arange(ns2) % SSM_TILE_STATES) // SSM_STATE

    bf16 = jnp.bfloat16

    def rows_in(w_re, w_im):
        w = jnp.stack([w_re, w_im]).reshape(2, SSM_TILES, tg, SSM_STATE, SSM_GROUP)
        return w.transpose(1, 4, 0, 2, 3).reshape(SSM_TILES, SSM_GROUP, ns2).astype(bf16)

    def cols_out(w_re, w_im):
        w = jnp.stack([w_re, -w_im]).reshape(2, SSM_TILES, tg, SSM_GROUP, SSM_STATE)
        return w.transpose(1, 0, 2, 4, 3).reshape(SSM_TILES, ns2, SSM_GROUP).astype(bf16)

    a_re_b, a_im_b = ab_re[..., None], ab_im[..., None]
    w_in = jnp.stack([rows_in(a_re_b * bb_re - a_im_b * bb_im, a_re_b * bb_im + a_im_b * bb_re),
                      rows_in(bb_re, bb_im)], axis=1)
    own_group = jnp.arange(tg)[:, None, None] == col_group[None, None, :]
    w_in = jnp.where(own_group, w_in[:, :, None], jnp.zeros((), bf16))
    w_in = w_in.reshape(SSM_TILES, 2 * SSM_TILE_CH, ns2)
    a_re_c, a_im_c = ab_re[:, None, :], ab_im[:, None, :]
    w_out = jnp.stack([cols_out(cr, ci),
                       cols_out(cr * a_re_c - ci * a_im_c, cr * a_im_c + ci * a_re_c)], axis=1)
    w_out = jnp.tile(w_out, (1, 1, 1, tg))
    w_out = jnp.where(col_group[:, None] == ch_group[None, :], w_out, jnp.zeros((), bf16))
    direct = (jnp.einsum("gcp,gpd->gdc", cr, bb_re, precision="highest")
              - jnp.einsum("gcp,gpd->gdc", ci, bb_im, precision="highest"))
    direct = direct.reshape(SSM_TILES, tg, SSM_GROUP, SSM_GROUP)
    k0 = jnp.einsum("ab,qadc->qadbc", jnp.eye(tg, dtype=f32), direct).reshape(
        SSM_TILES, SSM_TILE_CH, SSM_TILE_CH)

    def square(z):
        return (z[0] * z[0] - z[1] * z[1], 2.0 * z[0] * z[1])

    a2 = square((ab_re, ab_im))
    p = a2
    n = 2
    while n < TIME_SEG:
        p = square(p)
        n *= 2
    assert n == TIME_SEG

    def per_sublane(z):
        z = z.reshape(SSM_TILES, 1, SSM_TILE_STATES)
        return jnp.broadcast_to(z, (SSM_TILES, SUBLANES, SSM_TILE_STATES))

    sub = jnp.arange(SUBLANES)[None, :, None]
    tables = [per_sublane(z) for z in a2]
    for shift in (1, 2, 4):
        tables += [jnp.where(sub >= shift, per_sublane(z), 0.0) for z in p]
        p = square(p)
    ap = jnp.stack(tables, axis=1)
    return w_in, w_out, k0.astype(bf16), ap


def _ssm_out_kernel(y_ref, wv_ref, wg_ref, x_ref, o_ref, yb_ref):
    @pl.when(pl.program_id(1) == 0)
    def _():
        yb_ref[...] = jax.nn.gelu(y_ref[...]).astype(jnp.bfloat16)

    y = yb_ref[...]
    val = jnp.dot(y, wv_ref[...], preferred_element_type=jnp.float32)
    gate = jnp.dot(y, wg_ref[...], preferred_element_type=jnp.float32)
    o_ref[...] = x_ref[...] + val * jax.nn.sigmoid(gate)


def _ssm_out(y, w_glu, x):
    n_col = D_MODEL // TN
    return pl.pallas_call(
        _ssm_out_kernel,
        grid=(SEQ // TM, n_col),
        in_specs=[
            pl.BlockSpec((TM, D_MODEL), lambda i, n: (i, 0)),
            pl.BlockSpec((D_MODEL, TN), lambda i, n: (0, n)),
            pl.BlockSpec((D_MODEL, TN), lambda i, n: (0, n + n_col)),
            pl.BlockSpec((TM, TN), lambda i, n: (i, n)),
        ],
        out_specs=pl.BlockSpec((TM, TN), lambda i, n: (i, n)),
        out_shape=jax.ShapeDtypeStruct((SEQ, D_MODEL), jnp.float32),
        scratch_shapes=[pltpu.VMEM((TM, D_MODEL), jnp.bfloat16)],
        compiler_params=_params("parallel", "arbitrary"),
        name="ssm_out",
    )(y, w_glu, w_glu, x)


def _mlp_kernel(norm_out, n_casts, x_ref, g_ref, wu_ref, wd_ref, *rest):
    rest = list(rest)
    out_gain_ref = rest.pop(0) if norm_out else None
    cast_src = [rest.pop(0) for _ in range(n_casts)]
    o_ref = rest.pop(0)
    operand_refs = [rest.pop(0) for _ in range(3)] if norm_out == "operand" else None
    cast_dst = [rest.pop(0) for _ in range(n_casts)]
    (h_ref,) = rest
    f = pl.program_id(1)

    @pl.when(f == 0)
    def _():
        x = x_ref[...]
        h_ref[...] = _rms(x, g_ref[...]).astype(jnp.bfloat16)
        o_ref[...] = x

    a = jnp.maximum(jnp.dot(h_ref[...], wu_ref[...], preferred_element_type=jnp.float32), 0.0)
    a = (a * a).astype(jnp.bfloat16)
    o_ref[...] += jnp.dot(a, wd_ref[...], preferred_element_type=jnp.float32)

    for src_ref, dst_ref in zip(cast_src, cast_dst):
        dst_ref[...] = src_ref[...].astype(jnp.bfloat16)

    if norm_out:
        @pl.when(f == pl.num_programs(1) - 1)
        def _():
            y = o_ref[...]
            rstd = lax.rsqrt(jnp.mean(y * y, axis=-1, keepdims=True) + EPS)
            normed = y * rstd * out_gain_ref[...]
            if norm_out == "final":
                o_ref[...] = normed
            else:
                even_ref, odd_ref, rstd_out_ref = operand_refs
                groups = [normed[r * SUBLANES:(r + 1) * SUBLANES] for r in range(TM_MLP // SUBLANES)]
                even_ref[...] = jnp.concatenate(groups[0::2], axis=0).astype(jnp.bfloat16)
                odd_ref[...] = jnp.concatenate(groups[1::2], axis=0).astype(jnp.bfloat16)
                rstd_out_ref[...] = jnp.broadcast_to(rstd, rstd_out_ref.shape)


def _mlp(x, g, w_up, w_down, norm_out=None, out_gain=None, side_casts=()):
    assert (norm_out is None) == (out_gain is None)
    n_rows, n_ff = SEQ // TM_MLP, D_FF // TF_MLP
    gain_spec = pl.BlockSpec((1, D_MODEL), lambda i, f: (0, 0))
    in_specs = [
        pl.BlockSpec((TM_MLP, D_MODEL), lambda i, f: (i, 0)),
        gain_spec,
        pl.BlockSpec((D_MODEL, TF_MLP), lambda i, f: (0, f)),
        pl.BlockSpec((TF_MLP, D_MODEL), lambda i, f: (f, 0)),
    ]
    args = [x, g, w_up, w_down]
    out_specs = [pl.BlockSpec((TM_MLP, D_MODEL), lambda i, f: (i, 0))]
    out_shape = [jax.ShapeDtypeStruct((SEQ, D_MODEL), jnp.float32)]
    if norm_out:
        in_specs.append(gain_spec)
        args.append(out_gain)
    if norm_out == "operand":
        half = pl.BlockSpec((TM_MLP // 2, D_MODEL), lambda i, f: (i, 0))
        out_specs += [half, half, pl.BlockSpec((TM_MLP, LANES), lambda i, f: (i, 0))]
        out_shape += [jax.ShapeDtypeStruct((SEQ // 2, D_MODEL), jnp.bfloat16)] * 2
        out_shape += [jax.ShapeDtypeStruct((SEQ, LANES), jnp.float32)]
    for w, layer in side_casts:
        tile = (w.shape[1] // n_rows, w.shape[2] // n_ff)
        in_specs.append(pl.BlockSpec((None,) + tile, lambda i, f, layer=layer: (layer, i, f)))
        args.append(w)
        out_specs.append(pl.BlockSpec(tile, lambda i, f: (i, f)))
        out_shape.append(jax.ShapeDtypeStruct(w.shape[1:], jnp.bfloat16))
    outs = pl.pallas_call(
        functools.partial(_mlp_kernel, norm_out, len(side_casts)),
        grid=(n_rows, n_ff),
        in_specs=in_specs,
        out_specs=out_specs,
        out_shape=out_shape,
        scratch_shapes=[pltpu.VMEM((TM_MLP, D_MODEL), jnp.bfloat16)],
        compiler_params=_params("parallel", "arbitrary"),
        name="mlp",
    )(*args)
    return outs if len(outs) > 1 else outs[0]


def kernel(x, mix_norm, conv_w_in, conv_b_in, conv_dw, conv_dw_b, conv_ln_g, conv_ln_b, conv_w_out, conv_b_out, ssm_lambda_re, ssm_lambda_im, ssm_log_dt, ssm_b_re, ssm_b_im, ssm_c_re, ssm_c_im, ssm_d, ssm_w_glu, mlp_norm, mlp_w_up, mlp_w_down, final_norm):
    assert x.shape == (1, SEQ, D_MODEL) and x.dtype == jnp.float32
    bf16 = jnp.bfloat16
    n_blocks = SEQ // TIME_BLOCK
    x = x.reshape(n_blocks, SUBLANES, TIME_SEG, D_MODEL).transpose(0, 2, 1, 3).reshape(SEQ, D_MODEL)
    w_mix = [conv_w_in[0].astype(bf16), conv_w_out[0].astype(bf16)]
    w_up, w_down = mlp_w_up[0].astype(bf16), mlp_w_down[0].astype(bf16)
    for i in range(DEPTH):
        j = i // 2
        g = mix_norm[i].reshape(1, D_MODEL)
        if i % 2 == 0:
            w_in, w_out = w_mix
            v = _conv_in(x, g, w_in, conv_b_in[j].reshape(1, 2 * D_MODEL))
            v = _dwconv(v, conv_dw[j], conv_dw_b[j].reshape(1, D_MODEL),
                        conv_ln_g[j].reshape(1, D_MODEL), conv_ln_b[j].reshape(1, D_MODEL))
            x = _conv_out(v, w_out, conv_b_out[j].reshape(1, D_MODEL), x)
        else:
            (w_glu,) = w_mix
            tables = _ssm_tables(ssm_lambda_re[j], ssm_lambda_im[j], ssm_log_dt[j],
                                 ssm_b_re[j], ssm_b_im[j], ssm_c_re[j], ssm_c_im[j])
            y = _ssm_core(h_even, h_odd, x, rstd, ssm_d[j].reshape(1, D_MODEL) * g, *tables)
            x = _ssm_out(y, w_glu, x)
        g = mlp_norm[i].reshape(1, D_MODEL)
        if i == DEPTH - 1:
            x = _mlp(x, g, w_up, w_down, norm_out="final", out_gain=final_norm.reshape(1, D_MODEL))
            continue
        nxt = i + 1
        casts = [(mlp_w_up, nxt), (mlp_w_down, nxt)]
        if nxt % 2 == 0:
            casts += [(conv_w_in, nxt // 2), (conv_w_out, nxt // 2)]
            x, w_up, w_down, *w_mix = _mlp(x, g, w_up, w_down, side_casts=casts)
        else:
            casts += [(ssm_w_glu, nxt // 2)]
            x, h_even, h_odd, rstd, w_up, w_down, *w_mix = _mlp(
                x, g, w_up, w_down, norm_out="operand", out_gain=mix_norm[nxt].reshape(1, D_MODEL),
                side_casts=casts)
    x = x.reshape(n_blocks, TIME_SEG, SUBLANES, D_MODEL).transpose(0, 2, 1, 3)
    return x.reshape(1, SEQ, D_MODEL)
```

```python
import functools

import jax
import jax.numpy as jnp
from jax import lax
from jax.experimental import pallas as pl
from jax.experimental.pallas import tpu as pltpu

D_MODEL = 2048
SEQ = 8192
DEPTH = 4
CONV_WIDTH = 31
SSM_GROUP = 16
SSM_GROUPS = D_MODEL // SSM_GROUP
SSM_STATE = 64
D_FF = 4 * D_MODEL
EPS = 1e-6

SUBLANES = 8
LANES = 128
VMEM_LIMIT_BYTES = 56 * 1024 * 1024

TM = 1024
TN = 512
TM_MLP = 512
TF_MLP = 1024

TIME_BLOCK = 512
TIME_SEG = TIME_BLOCK // SUBLANES

CONV_HALO = 32 * SUBLANES
CONV_ROWS = 128
LN_ROWS = 16
LN_UNROLL = 4

SSM_TILE_GROUPS = 16
SSM_TILE_CH = SSM_TILE_GROUPS * SSM_GROUP
SSM_TILE_STATES = SSM_TILE_GROUPS * SSM_STATE
SSM_TILES = SSM_GROUPS // SSM_TILE_GROUPS
SSM_IN_COLS = 256


def _params(*sem):
    return pltpu.CompilerParams(dimension_semantics=sem, vmem_limit_bytes=VMEM_LIMIT_BYTES)


def _rms(x, g):
    ms = jnp.mean(x * x, axis=-1, keepdims=True)
    return x * lax.rsqrt(ms + EPS) * g


def _side_cast_specs(side_casts, n_rows, n_cols):
    in_specs, args, out_specs, out_shapes = [], [], [], []
    for w, layer in side_casts:
        tile = (w.shape[1] // n_rows, w.shape[2] // n_cols)
        in_specs.append(pl.BlockSpec((None,) + tile, lambda i, n, layer=layer: (layer, i, n)))
        args.append(w)
        out_specs.append(pl.BlockSpec(tile, lambda i, n: (i, n)))
        out_shapes.append(jax.ShapeDtypeStruct(w.shape[1:], jnp.bfloat16))
    return in_specs, args, out_specs, out_shapes


def _run_side_casts(src_refs, dst_refs):
    for src_ref, dst_ref in zip(src_refs, dst_refs):
        dst_ref[...] = src_ref[...].astype(jnp.bfloat16)


def _conv_in_kernel(n_casts, x_ref, g_ref, wa_ref, wg_ref, ba_ref, bg_ref, *rest):
    cast_src, (o_ref, *cast_dst, h_ref) = rest[:n_casts], rest[n_casts:]

    @pl.when(pl.program_id(1) == 0)
    def _():
        h_ref[...] = _rms(x_ref[...], g_ref[...]).astype(jnp.bfloat16)

    h = h_ref[...]
    a = jnp.dot(h, wa_ref[...], preferred_element_type=jnp.float32) + ba_ref[...]
    gate = jnp.dot(h, wg_ref[...], preferred_element_type=jnp.float32) + bg_ref[...]
    v = a * jax.nn.sigmoid(gate)
    for j in range(TN // LANES):
        o_ref[j] = v[:, j * LANES:(j + 1) * LANES]
    _run_side_casts(cast_src, cast_dst)


def _conv_in(x, g, w_in, b_in, side_casts=()):
    n_col = D_MODEL // TN
    c_in, c_args, c_out, c_shapes = _side_cast_specs(side_casts, SEQ // TM, n_col)
    outs = pl.pallas_call(
        functools.partial(_conv_in_kernel, len(side_casts)),
        grid=(SEQ // TM, n_col),
        in_specs=[
            pl.BlockSpec((TM, D_MODEL), lambda i, n: (i, 0)),
            pl.BlockSpec((1, D_MODEL), lambda i, n: (0, 0)),
            pl.BlockSpec((D_MODEL, TN), lambda i, n: (0, n)),
            pl.BlockSpec((D_MODEL, TN), lambda i, n: (0, n + n_col)),
            pl.BlockSpec((1, TN), lambda i, n: (0, n)),
            pl.BlockSpec((1, TN), lambda i, n: (0, n + n_col)),
        ] + c_in,
        out_specs=[pl.BlockSpec((TN // LANES, TM, LANES), lambda i, n: (n, i, 0))] + c_out,
        out_shape=[jax.ShapeDtypeStruct((D_MODEL // LANES, SEQ, LANES), jnp.float32)] + c_shapes,
        scratch_shapes=[pltpu.VMEM((TM, D_MODEL), jnp.bfloat16)],
        compiler_params=_params("parallel", "arbitrary"),
        name="conv_in",
    )(x, g, w_in, w_in, b_in, b_in, *c_args)
    return outs if side_casts else outs[0]


def _dwconv_kernel(v_ref, dw_ref, dwb_ref, lg_ref, lb_ref, o_ref, ext_ref, tail_ref, cv_ref):
    @pl.when(pl.program_id(0) == 0)
    def _():
        tail_ref[...] = jnp.zeros(tail_ref.shape, jnp.float32)

    tail_start = TIME_BLOCK - CONV_HALO
    first_tap = CONV_HALO - (CONV_WIDTH - 1) * SUBLANES
    first_segment = lax.broadcasted_iota(jnp.int32, (SUBLANES, LANES), 0) == 0

    def conv_lane_tile(c, carry):
        ext_ref[c, CONV_HALO:, :] = v_ref[c]
        for m in range(CONV_HALO // SUBLANES):
            rows = slice(m * SUBLANES, (m + 1) * SUBLANES)
            cur = pltpu.roll(v_ref[c, tail_start + m * SUBLANES:tail_start + (m + 1) * SUBLANES, :], 1, 0)
            prev = pltpu.roll(tail_ref[c, rows, :], 1, 0)
            ext_ref[c, rows, :] = jnp.where(first_segment, prev, cur)
        tail_ref[c] = v_ref[c, tail_start:, :]
        for r in range(TIME_BLOCK // CONV_ROWS):
            acc = jnp.zeros((CONV_ROWS, LANES), jnp.float32)
            for k in range(CONV_WIDTH):
                start = r * CONV_ROWS + first_tap + k * SUBLANES
                w = dw_ref[c, k * SUBLANES:(k + 1) * SUBLANES, :]
                acc = acc + ext_ref[c, start:start + CONV_ROWS, :] * jnp.concatenate(
                    [w] * (CONV_ROWS // SUBLANES), axis=0)
            cv_ref[c, r * CONV_ROWS:(r + 1) * CONV_ROWS, :] = acc + dwb_ref[c]
        return carry

    lax.fori_loop(0, D_MODEL // LANES, conv_lane_tile, 0)

    def norm_rows(r, carry):
        r0 = pl.multiple_of(r * LN_ROWS, LN_ROWS)
        c = cv_ref[:, pl.ds(r0, LN_ROWS), :]
        mu = jnp.sum(jnp.sum(c, axis=0), axis=-1, keepdims=True) * (1.0 / D_MODEL)
        xc = c - mu
        var = jnp.sum(jnp.sum(xc * xc, axis=0), axis=-1, keepdims=True) * (1.0 / D_MODEL)
        y = xc * lax.rsqrt(var + EPS) * lg_ref[...] + lb_ref[...]
        y = (y * jax.nn.sigmoid(y)).astype(jnp.bfloat16)
        for j in range(D_MODEL // LANES):
            o_ref[pl.ds(r0, LN_ROWS), j * LANES:(j + 1) * LANES] = y[j]
        return carry

    lax.fori_loop(0, TIME_BLOCK // LN_ROWS, norm_rows, 0, unroll=LN_UNROLL)


def _lane_tiles(p):
    p = p.reshape(-1, D_MODEL // LANES, LANES)
    return p.transpose(1, 0, 2)


def _dwconv(v, dw, dw_b, ln_g, ln_b):
    n_lt = D_MODEL // LANES
    whole = lambda i: (0, 0, 0)
    dw = _lane_tiles(jnp.repeat(dw, SUBLANES, axis=0))
    return pl.pallas_call(
        _dwconv_kernel,
        grid=(SEQ // TIME_BLOCK,),
        in_specs=[
            pl.BlockSpec((n_lt, TIME_BLOCK, LANES), lambda i: (0, i, 0)),
            pl.BlockSpec((n_lt, CONV_WIDTH * SUBLANES, LANES), whole),
            pl.BlockSpec((n_lt, 1, LANES), whole),
            pl.BlockSpec((n_lt, 1, LANES), whole),
            pl.BlockSpec((n_lt, 1, LANES), whole),
        ],
        out_specs=pl.BlockSpec((TIME_BLOCK, D_MODEL), lambda i: (i, 0)),
        out_shape=jax.ShapeDtypeStruct((SEQ, D_MODEL), jnp.bfloat16),
        scratch_shapes=[
            pltpu.VMEM((n_lt, CONV_HALO + TIME_BLOCK, LANES), jnp.float32),
            pltpu.VMEM((n_lt, CONV_HALO, LANES), jnp.float32),
            pltpu.VMEM((n_lt, TIME_BLOCK, LANES), jnp.float32),
        ],
        compiler_params=_params("arbitrary"),
        name="dwconv_ln_silu",
    )(v, dw, _lane_tiles(dw_b), _lane_tiles(ln_g), _lane_tiles(ln_b))


def _conv_out_kernel(n_casts, v_ref, w_ref, b_ref, x_ref, *rest):
    cast_src, (o_ref, *cast_dst) = rest[:n_casts], rest[n_casts:]
    y = jnp.dot(v_ref[...], w_ref[...], preferred_element_type=jnp.float32)
    o_ref[...] = x_ref[...] + (y + b_ref[...])
    _run_side_casts(cast_src, cast_dst)


def _conv_out(v, w_out, b_out, x, side_casts=()):
    n_col = D_MODEL // TN
    c_in, c_args, c_out, c_shapes = _side_cast_specs(side_casts, SEQ // TM, n_col)
    outs = pl.pallas_call(
        functools.partial(_conv_out_kernel, len(side_casts)),
        grid=(SEQ // TM, n_col),
        in_specs=[
            pl.BlockSpec((TM, D_MODEL), lambda i, n: (i, 0)),
            pl.BlockSpec((D_MODEL, TN), lambda i, n: (0, n)),
            pl.BlockSpec((1, TN), lambda i, n: (0, n)),
            pl.BlockSpec((TM, TN), lambda i, n: (i, n)),
        ] + c_in,
        out_specs=[pl.BlockSpec((TM, TN), lambda i, n: (i, n))] + c_out,
        out_shape=[jax.ShapeDtypeStruct((SEQ, D_MODEL), jnp.float32)] + c_shapes,
        compiler_params=_params("parallel", "parallel"),
        name="conv_out",
    )(v, w_out, b_out, x, *c_args)
    return outs if side_casts else outs[0]


def _ssm_stages(h_in_ref, skip_refs, bw_ref, cw_ref, ap_ref, o_ref, carry_ref, keep_carry,
                bu_w_ref, bu_r_ref, s_w_ref, s_r_ref):
    ns = SSM_TILE_STATES
    x_ref, rstd_ref, dg_ref = skip_refs

    def project_in(n):
        cols = slice(n * SSM_IN_COLS, (n + 1) * SSM_IN_COLS)
        bu_w_ref[:, cols] = jnp.dot(h_in_ref[...], bw_ref[0, :, cols], preferred_element_type=jnp.float32)

    a_re = ap_ref[0, 0]
    a_im = ap_ref[0, 1]

    def advance(i, s_re, s_im):
        rows = slice(i * SUBLANES, (i + 1) * SUBLANES)
        b_re = bu_r_ref[rows, 0:ns]
        b_im = bu_r_ref[rows, ns:2 * ns]
        return (a_re * s_re - a_im * s_im + b_re, a_re * s_im + a_im * s_re + b_im)

    zero = jnp.zeros((SUBLANES, ns), jnp.float32)
    in_re = carry_ref[0]
    in_im = carry_ref[1]
    f_re, f_im = in_re, in_im
    n_in = 2 * ns // SSM_IN_COLS
    for i in range(TIME_SEG):
        f_re, f_im = advance(i, f_re, f_im)
        if i % (TIME_SEG // n_in) == 0:
            project_in(i // (TIME_SEG // n_in))

    for level, shift in enumerate((1, 2, 4)):
        p_re = ap_ref[0, 2 + 2 * level]
        p_im = ap_ref[0, 3 + 2 * level]
        r_re = pltpu.roll(f_re, shift, 0)
        r_im = pltpu.roll(f_im, shift, 0)
        f_re, f_im = (f_re + (p_re * r_re - p_im * r_im), f_im + (p_re * r_im + p_im * r_re))

    row = lax.broadcasted_iota(jnp.int32, (SUBLANES, ns), 0)
    e_re = pltpu.roll(f_re, 1, 0)
    e_im = pltpu.roll(f_im, 1, 0)
    hand_over = (row == 0) & keep_carry
    carry_ref[0] = jnp.where(hand_over, e_re, zero)
    carry_ref[1] = jnp.where(hand_over, e_im, zero)
    s_re = jnp.where(row == 0, in_re, e_re)
    s_im = jnp.where(row == 0, in_im, e_im)

    o_ref[...] = jnp.dot(s_r_ref[...], cw_ref[0], preferred_element_type=jnp.float32)

    for i2 in range(TIME_SEG // 2):
        m_re, m_im = advance(2 * i2, s_re, s_im)
        s_re, s_im = advance(2 * i2 + 1, m_re, m_im)
        rows = slice(i2 * 2 * SUBLANES, (i2 + 1) * 2 * SUBLANES)
        s_w_ref[rows, 0:ns] = jnp.concatenate([m_re, s_re], axis=0).astype(jnp.bfloat16)
        s_w_ref[rows, ns:2 * ns] = jnp.concatenate([m_im, s_im], axis=0).astype(jnp.bfloat16)

    rstd = jnp.concatenate([rstd_ref[...]] * (SSM_TILE_CH // LANES), axis=1)
    o_ref[...] += dg_ref[...] * (x_ref[...] * rstd)


def _ssm_kernel(h_in_ref, x_ref, bw_ref, cw_ref, ap_ref, dg_ref, rstd_ref, o_ref,
                bu0_ref, bu1_ref, s0_ref, s1_ref, carry_ref):
    g = pl.program_id(0)

    @pl.when(g == 0)
    def _():
        for ref in (bu0_ref, bu1_ref, s0_ref, s1_ref, carry_ref):
            ref[...] = jnp.zeros(ref.shape, ref.dtype)

    keep_carry = g % (SEQ // TIME_BLOCK) != 0
    stages = functools.partial(_ssm_stages, h_in_ref, (x_ref, rstd_ref, dg_ref), bw_ref, cw_ref,
                               ap_ref, o_ref, carry_ref, keep_carry)

    @pl.when(g % 2 == 0)
    def _():
        stages(bu0_ref, bu1_ref, s1_ref, s0_ref)

    @pl.when(g % 2 == 1)
    def _():
        stages(bu1_ref, bu0_ref, s0_ref, s1_ref)


def _ssm_core(h_bf16, x, rstd, d_gain, bw, cw, ap):
    ns = SSM_TILE_STATES
    n_time = SEQ // TIME_BLOCK
    n_blocks = SSM_TILES * n_time

    def block(g, lag):
        b = jnp.clip(g - lag, 0, n_blocks - 1)
        return b % n_time, b // n_time

    def h_map(lag):
        return lambda g: block(g, lag)

    def tile_map(lag, rank):
        return lambda g: (block(g, lag)[1],) + (0,) * (rank - 1)

    return pl.pallas_call(
        _ssm_kernel,
        grid=(n_blocks + 2,),
        in_specs=[
            pl.BlockSpec((TIME_BLOCK, SSM_TILE_CH), h_map(0)),
            pl.BlockSpec((TIME_BLOCK, SSM_TILE_CH), h_map(2)),
            pl.BlockSpec((1, SSM_TILE_CH, 2 * ns), tile_map(0, 3)),
            pl.BlockSpec((1, 2 * ns, SSM_TILE_CH), tile_map(2, 3)),
            pl.BlockSpec((1, 8, SUBLANES, ns), tile_map(1, 4)),
            pl.BlockSpec((1, SSM_TILE_CH), lambda g: (0, block(g, 2)[1])),
            pl.BlockSpec((TIME_BLOCK, LANES), lambda g: (block(g, 2)[0], 0)),
        ],
        out_specs=pl.BlockSpec((TIME_BLOCK, SSM_TILE_CH), h_map(2)),
        out_shape=jax.ShapeDtypeStruct((SEQ, D_MODEL), jnp.float32),
        scratch_shapes=[
            pltpu.VMEM((TIME_BLOCK, 2 * ns), jnp.float32),
            pltpu.VMEM((TIME_BLOCK, 2 * ns), jnp.float32),
            pltpu.VMEM((TIME_BLOCK, 2 * ns), jnp.bfloat16),
            pltpu.VMEM((TIME_BLOCK, 2 * ns), jnp.bfloat16),
            pltpu.VMEM((2, SUBLANES, ns), jnp.float32),
        ],
        compiler_params=_params("arbitrary"),
        name="ssm_scan",
    )(h_bf16, x, bw, cw, ap, d_gain, rstd)


def _ssm_tables(lam_re, lam_im, log_dt, b_re, b_im, c_re, c_im):
    f32 = jnp.float32
    dt = jnp.exp(log_dt.astype(f32))[:, None]
    lr = lam_re.astype(f32)
    li = lam_im.astype(f32)
    mag = jnp.exp(lr * dt)
    ab_re = mag * jnp.cos(li * dt)
    ab_im = mag * jnp.sin(li * dt)
    nr = ab_re - 1.0
    ni = ab_im
    den = lr * lr + li * li
    k_re = ((nr * lr + ni * li) / den)[..., None]
    k_im = ((ni * lr - nr * li) / den)[..., None]
    br = b_re.astype(f32)
    bi = b_im.astype(f32)
    bb = jnp.stack([k_re * br - k_im * bi, k_re * bi + k_im * br])
    cc = jnp.stack([c_re.astype(f32), -c_im.astype(f32)])

    tg = SSM_TILE_GROUPS
    ns2 = 2 * SSM_TILE_STATES
    bb = bb.reshape(2, SSM_TILES, tg, SSM_STATE, SSM_GROUP).transpose(1, 4, 0, 2, 3)
    cc = cc.reshape(2, SSM_TILES, tg, SSM_GROUP, SSM_STATE).transpose(1, 0, 2, 4, 3)
    bb = jnp.tile(bb.reshape(SSM_TILES, SSM_GROUP, ns2), (1, tg, 1))
    cc = jnp.tile(cc.reshape(SSM_TILES, ns2, SSM_GROUP), (1, 1, tg))
    ch_group = jnp.arange(SSM_TILE_CH) // SSM_GROUP
    col_group = (jnp.arange(ns2) % SSM_TILE_STATES) // SSM_STATE
    bw = jnp.where(ch_group[:, None] == col_group[None, :], bb, 0.0)
    cw = jnp.where(col_group[:, None] == ch_group[None, :], cc, 0.0)

    def square(z):
        return (z[0] * z[0] - z[1] * z[1], 2.0 * z[0] * z[1])

    a = (ab_re, ab_im)
    p = a
    n = 1
    while n < TIME_SEG:
        p = square(p)
        n *= 2
    assert n == TIME_SEG

    def per_sublane(z):
        z = z.reshape(SSM_TILES, 1, SSM_TILE_STATES)
        return jnp.broadcast_to(z, (SSM_TILES, SUBLANES, SSM_TILE_STATES))

    sub = jnp.arange(SUBLANES)[None, :, None]
    tables = [per_sublane(z) for z in a]
    for shift in (1, 2, 4):
        tables += [jnp.where(sub >= shift, per_sublane(z), 0.0) for z in p]
        p = square(p)
    ap = jnp.stack(tables, axis=1)
    return bw.astype(jnp.bfloat16), cw.astype(jnp.bfloat16), ap


def _ssm_out_kernel(y_ref, wv_ref, wg_ref, x_ref, o_ref, yb_ref):
    @pl.when(pl.program_id(1) == 0)
    def _():
        yb_ref[...] = jax.nn.gelu(y_ref[...]).astype(jnp.bfloat16)

    y = yb_ref[...]
    val = jnp.dot(y, wv_ref[...], preferred_element_type=jnp.float32)
    gate = jnp.dot(y, wg_ref[...], preferred_element_type=jnp.float32)
    o_ref[...] = x_ref[...] + val * jax.nn.sigmoid(gate)


def _ssm_out(y, w_glu, x):
    n_col = D_MODEL // TN
    return pl.pallas_call(
        _ssm_out_kernel,
        grid=(SEQ // TM, n_col),
        in_specs=[
            pl.BlockSpec((TM, D_MODEL), lambda i, n: (i, 0)),
            pl.BlockSpec((D_MODEL, TN), lambda i, n: (0, n)),
            pl.BlockSpec((D_MODEL, TN), lambda i, n: (0, n + n_col)),
            pl.BlockSpec((TM, TN), lambda i, n: (i, n)),
        ],
        out_specs=pl.BlockSpec((TM, TN), lambda i, n: (i, n)),
        out_shape=jax.ShapeDtypeStruct((SEQ, D_MODEL), jnp.float32),
        scratch_shapes=[pltpu.VMEM((TM, D_MODEL), jnp.bfloat16)],
        compiler_params=_params("parallel", "arbitrary"),
        name="ssm_out",
    )(y, w_glu, w_glu, x)


def _mlp_kernel(norm_out, n_casts, x_ref, g_ref, wu_ref, wd_ref, *rest):
    rest = list(rest)
    out_gain_ref = rest.pop(0) if norm_out else None
    cast_src = [rest.pop(0) for _ in range(n_casts)]
    o_ref = rest.pop(0)
    hb_out_ref, rstd_out_ref = (rest.pop(0), rest.pop(0)) if norm_out == "operand" else (None, None)
    cast_dst = [rest.pop(0) for _ in range(n_casts)]
    (h_ref,) = rest
    f = pl.program_id(1)

    @pl.when(f == 0)
    def _():
        x = x_ref[...]
        h_ref[...] = _rms(x, g_ref[...]).astype(jnp.bfloat16)
        o_ref[...] = x

    a = jnp.maximum(jnp.dot(h_ref[...], wu_ref[...], preferred_element_type=jnp.float32), 0.0)
    a = (a * a).astype(jnp.bfloat16)
    o_ref[...] += jnp.dot(a, wd_ref[...], preferred_element_type=jnp.float32)
    _run_side_casts(cast_src, cast_dst)

    if norm_out:
        @pl.when(f == pl.num_programs(1) - 1)
        def _():
            y = o_ref[...]
            rstd = lax.rsqrt(jnp.mean(y * y, axis=-1, keepdims=True) + EPS)
            normed = y * rstd * out_gain_ref[...]
            if norm_out == "final":
                o_ref[...] = normed
            else:
                hb_out_ref[...] = normed.astype(jnp.bfloat16)
                rstd_out_ref[...] = jnp.broadcast_to(rstd, rstd_out_ref.shape)


def _mlp(x, g, w_up, w_down, norm_out=None, out_gain=None, side_casts=()):
    assert (norm_out is None) == (out_gain is None)
    n_rows, n_ff = SEQ // TM_MLP, D_FF // TF_MLP
    gain_spec = pl.BlockSpec((1, D_MODEL), lambda i, f: (0, 0))
    in_specs = [
        pl.BlockSpec((TM_MLP, D_MODEL), lambda i, f: (i, 0)),
        gain_spec,
        pl.BlockSpec((D_MODEL, TF_MLP), lambda i, f: (0, f)),
        pl.BlockSpec((TF_MLP, D_MODEL), lambda i, f: (f, 0)),
    ]
    args = [x, g, w_up, w_down]
    out_specs = [pl.BlockSpec((TM_MLP, D_MODEL), lambda i, f: (i, 0))]
    out_shape = [jax.ShapeDtypeStruct((SEQ, D_MODEL), jnp.float32)]
    if norm_out:
        in_specs.append(gain_spec)
        args.append(out_gain)
    if norm_out == "operand":
        out_specs += [pl.BlockSpec((TM_MLP, D_MODEL), lambda i, f: (i, 0)),
                      pl.BlockSpec((TM_MLP, LANES), lambda i, f: (i, 0))]
        out_shape += [jax.ShapeDtypeStruct((SEQ, D_MODEL), jnp.bfloat16),
                      jax.ShapeDtypeStruct((SEQ, LANES), jnp.float32)]
    c_in, c_args, c_out, c_shapes = _side_cast_specs(side_casts, n_rows, n_ff)
    in_specs, args, out_specs, out_shape = in_specs + c_in, args + c_args, out_specs + c_out, out_shape + c_shapes
    outs = pl.pallas_call(
        functools.partial(_mlp_kernel, norm_out, len(side_casts)),
        grid=(n_rows, n_ff),
        in_specs=in_specs,
        out_specs=out_specs,
        out_shape=out_shape,
        scratch_shapes=[pltpu.VMEM((TM_MLP, D_MODEL), jnp.bfloat16)],
        compiler_params=_params("parallel", "arbitrary"),
        name="mlp",
    )(*args)
    return outs if len(outs) > 1 else outs[0]


def kernel(x, mix_norm, conv_w_in, conv_b_in, conv_dw, conv_dw_b, conv_ln_g, conv_ln_b, conv_w_out, conv_b_out, ssm_lambda_re, ssm_lambda_im, ssm_log_dt, ssm_b_re, ssm_b_im, ssm_c_re, ssm_c_im, ssm_d, ssm_w_glu, mlp_norm, mlp_w_up, mlp_w_down, final_norm):
    assert x.shape == (1, SEQ, D_MODEL) and x.dtype == jnp.float32
    bf16 = jnp.bfloat16
    n_blocks = SEQ // TIME_BLOCK
    x = x.reshape(n_blocks, SUBLANES, TIME_SEG, D_MODEL).transpose(0, 2, 1, 3).reshape(SEQ, D_MODEL)
    w_mix = [conv_w_in[0].astype(bf16), conv_w_out[0].astype(bf16)]
    w_up = w_down = None
    for i in range(DEPTH):
        j = i // 2
        g = mix_norm[i].reshape(1, D_MODEL)
        if i % 2 == 0:
            w_in, w_out = w_mix
            v = _conv_in(x, g, w_in, conv_b_in[j].reshape(1, 2 * D_MODEL),
                         side_casts=[] if w_up is not None else [(mlp_w_up, i)])
            if w_up is None:
                v, w_up = v
            v = _dwconv(v, conv_dw[j], conv_dw_b[j].reshape(1, D_MODEL),
                        conv_ln_g[j].reshape(1, D_MODEL), conv_ln_b[j].reshape(1, D_MODEL))
            x = _conv_out(v, w_out, conv_b_out[j].reshape(1, D_MODEL), x,
                          side_casts=[] if w_down is not None else [(mlp_w_down, i)])
            if w_down is None:
                x, w_down = x
        else:
            (w_glu,) = w_mix
            bw, cw, ap = _ssm_tables(ssm_lambda_re[j], ssm_lambda_im[j], ssm_log_dt[j],
                                     ssm_b_re[j], ssm_b_im[j], ssm_c_re[j], ssm_c_im[j])
            y = _ssm_core(h_bf16, x, rstd, ssm_d[j].reshape(1, D_MODEL) * g, bw, cw, ap)
            x = _ssm_out(y, w_glu, x)
        g = mlp_norm[i].reshape(1, D_MODEL)
        if i == DEPTH - 1:
            x = _mlp(x, g, w_up, w_down, norm_out="final", out_gain=final_norm.reshape(1, D_MODEL))
            continue
        nxt = i + 1
        casts = [(mlp_w_up, nxt), (mlp_w_down, nxt)]
        if nxt % 2 == 0:
            casts += [(conv_w_in, nxt // 2), (conv_w_out, nxt // 2)]
            x, w_up, w_down, *w_mix = _mlp(x, g, w_up, w_down, side_casts=casts)
        else:
            casts += [(ssm_w_glu, nxt // 2)]
            x, h_bf16, rstd, w_up, w_down, *w_mix = _mlp(
                x, g, w_up, w_down, norm_out="operand", out_gain=mix_norm[nxt].reshape(1, D_MODEL),
                side_casts=casts)
    x = x.reshape(n_blocks, TIME_SEG, SUBLANES, D_MODEL).transpose(0, 2, 1, 3)
    return x.reshape(1, SEQ, D_MODEL)
```

```python
import functools

import jax
import jax.numpy as jnp
from jax import lax
from jax.experimental import pallas as pl
from jax.experimental.pallas import tpu as pltpu

D_MODEL = 2048
SEQ = 8192
DEPTH = 4
CONV_WIDTH = 31
SSM_GROUP = 16
SSM_GROUPS = D_MODEL // SSM_GROUP
SSM_STATE = 64
D_FF = 4 * D_MODEL
EPS = 1e-6

SUBLANES = 8
LANES = 128
VMEM_LIMIT_BYTES = 56 * 1024 * 1024

TM = 1024
TN = 512
TM_MLP = 512
TF_MLP = 1024

TIME_BLOCK = 512
TIME_SEG = TIME_BLOCK // SUBLANES

CONV_HALO = 32 * SUBLANES
CONV_ROWS = 128
LN_ROWS = 16
LN_UNROLL = 16

SSM_TILE_GROUPS = 16
SSM_TILE_CH = SSM_TILE_GROUPS * SSM_GROUP
SSM_TILE_STATES = SSM_TILE_GROUPS * SSM_STATE
SSM_TILES = SSM_GROUPS // SSM_TILE_GROUPS
SSM_IN_COLS = 256


def _params(*sem):
    return pltpu.CompilerParams(dimension_semantics=sem, vmem_limit_bytes=VMEM_LIMIT_BYTES)


def _rms(x, g):
    ms = jnp.mean(x * x, axis=-1, keepdims=True)
    return x * lax.rsqrt(ms + EPS) * g


def _side_cast_specs(side_casts, n_rows, n_cols=None):
    in_specs, args, out_specs, out_shapes = [], [], [], []
    for w, layer in side_casts:
        tile = (w.shape[1] // n_rows, w.shape[2] // (n_cols or 1))
        if n_cols is None:
            src_map, dst_map = (lambda i, layer=layer: (layer, i, 0)), (lambda i: (i, 0))
        else:
            src_map, dst_map = (lambda i, n, layer=layer: (layer, i, n)), (lambda i, n: (i, n))
        in_specs.append(pl.BlockSpec((None,) + tile, src_map))
        args.append(w)
        out_specs.append(pl.BlockSpec(tile, dst_map))
        out_shapes.append(jax.ShapeDtypeStruct(w.shape[1:], jnp.bfloat16))
    return in_specs, args, out_specs, out_shapes


def _run_side_casts(src_refs, dst_refs):
    for src_ref, dst_ref in zip(src_refs, dst_refs):
        dst_ref[...] = src_ref[...].astype(jnp.bfloat16)


def _conv_in_kernel(x_ref, g_ref, wa_ref, wg_ref, ba_ref, bg_ref, o_ref, h_ref):
    @pl.when(pl.program_id(1) == 0)
    def _():
        h_ref[...] = _rms(x_ref[...], g_ref[...]).astype(jnp.bfloat16)

    h = h_ref[...]
    a = jnp.dot(h, wa_ref[...], preferred_element_type=jnp.float32) + ba_ref[...]
    gate = jnp.dot(h, wg_ref[...], preferred_element_type=jnp.float32) + bg_ref[...]
    v = a * jax.nn.sigmoid(gate)
    for j in range(TN // LANES):
        o_ref[j] = v[:, j * LANES:(j + 1) * LANES]


def _conv_in(x, g, w_in, b_in):
    n_col = D_MODEL // TN
    return pl.pallas_call(
        _conv_in_kernel,
        grid=(SEQ // TM, n_col),
        in_specs=[
            pl.BlockSpec((TM, D_MODEL), lambda i, n: (i, 0)),
            pl.BlockSpec((1, D_MODEL), lambda i, n: (0, 0)),
            pl.BlockSpec((D_MODEL, TN), lambda i, n: (0, n)),
            pl.BlockSpec((D_MODEL, TN), lambda i, n: (0, n + n_col)),
            pl.BlockSpec((1, TN), lambda i, n: (0, n)),
            pl.BlockSpec((1, TN), lambda i, n: (0, n + n_col)),
        ],
        out_specs=pl.BlockSpec((TN // LANES, TM, LANES), lambda i, n: (n, i, 0)),
        out_shape=jax.ShapeDtypeStruct((D_MODEL // LANES, SEQ, LANES), jnp.float32),
        scratch_shapes=[pltpu.VMEM((TM, D_MODEL), jnp.bfloat16)],
        compiler_params=_params("parallel", "arbitrary"),
        name="conv_in",
    )(x, g, w_in, w_in, b_in, b_in)


def _dwconv_kernel(n_casts, v_ref, dw_ref, dwb_ref, lg_ref, lb_ref, *rest):
    cast_src, (o_ref, *cast_dst, ext_ref, tail_ref, cv_ref) = rest[:n_casts], rest[n_casts:]
    @pl.when(pl.program_id(0) == 0)
    def _():
        tail_ref[...] = jnp.zeros(tail_ref.shape, jnp.float32)

    tail_start = TIME_BLOCK - CONV_HALO
    first_tap = CONV_HALO - (CONV_WIDTH - 1) * SUBLANES
    first_segment = lax.broadcasted_iota(jnp.int32, (SUBLANES, LANES), 0) == 0

    def conv_lane_tile(c, carry):
        ext_ref[c, CONV_HALO:, :] = v_ref[c]
        for m in range(CONV_HALO // SUBLANES):
            rows = slice(m * SUBLANES, (m + 1) * SUBLANES)
            cur = pltpu.roll(v_ref[c, tail_start + m * SUBLANES:tail_start + (m + 1) * SUBLANES, :], 1, 0)
            prev = pltpu.roll(tail_ref[c, rows, :], 1, 0)
            ext_ref[c, rows, :] = jnp.where(first_segment, prev, cur)
        tail_ref[c] = v_ref[c, tail_start:, :]
        for r in range(TIME_BLOCK // CONV_ROWS):
            acc = jnp.zeros((CONV_ROWS, LANES), jnp.float32)
            for k in range(CONV_WIDTH):
                start = r * CONV_ROWS + first_tap + k * SUBLANES
                w = dw_ref[c, k * SUBLANES:(k + 1) * SUBLANES, :]
                acc = acc + ext_ref[c, start:start + CONV_ROWS, :] * jnp.concatenate(
                    [w] * (CONV_ROWS // SUBLANES), axis=0)
            cv_ref[c, r * CONV_ROWS:(r + 1) * CONV_ROWS, :] = acc + dwb_ref[c]
        return carry

    lax.fori_loop(0, D_MODEL // LANES, conv_lane_tile, 0)

    def norm_rows(r, carry):
        r0 = pl.multiple_of(r * LN_ROWS, LN_ROWS)
        c = cv_ref[:, pl.ds(r0, LN_ROWS), :]
        mu = jnp.sum(jnp.sum(c, axis=0), axis=-1, keepdims=True) * (1.0 / D_MODEL)
        xc = c - mu
        var = jnp.sum(jnp.sum(xc * xc, axis=0), axis=-1, keepdims=True) * (1.0 / D_MODEL)
        y = xc * lax.rsqrt(var + EPS) * lg_ref[...] + lb_ref[...]
        y = (y * jax.nn.sigmoid(y)).astype(jnp.bfloat16)
        for j in range(D_MODEL // LANES):
            o_ref[pl.ds(r0, LN_ROWS), j * LANES:(j + 1) * LANES] = y[j]
        return carry

    lax.fori_loop(0, TIME_BLOCK // LN_ROWS, norm_rows, 0, unroll=LN_UNROLL)
    _run_side_casts(cast_src, cast_dst)


def _lane_tiles(p):
    p = p.reshape(-1, D_MODEL // LANES, LANES)
    return p.transpose(1, 0, 2)


def _dwconv(v, dw, dw_b, ln_g, ln_b, side_casts=()):
    n_lt = D_MODEL // LANES
    whole = lambda i: (0, 0, 0)
    dw = _lane_tiles(jnp.repeat(dw, SUBLANES, axis=0))
    c_in, c_args, c_out, c_shapes = _side_cast_specs(side_casts, SEQ // TIME_BLOCK)
    outs = pl.pallas_call(
        functools.partial(_dwconv_kernel, len(side_casts)),
        grid=(SEQ // TIME_BLOCK,),
        in_specs=[
            pl.BlockSpec((n_lt, TIME_BLOCK, LANES), lambda i: (0, i, 0)),
            pl.BlockSpec((n_lt, CONV_WIDTH * SUBLANES, LANES), whole),
            pl.BlockSpec((n_lt, 1, LANES), whole),
            pl.BlockSpec((n_lt, 1, LANES), whole),
            pl.BlockSpec((n_lt, 1, LANES), whole),
        ] + c_in,
        out_specs=[pl.BlockSpec((TIME_BLOCK, D_MODEL), lambda i: (i, 0))] + c_out,
        out_shape=[jax.ShapeDtypeStruct((SEQ, D_MODEL), jnp.bfloat16)] + c_shapes,
        scratch_shapes=[
            pltpu.VMEM((n_lt, CONV_HALO + TIME_BLOCK, LANES), jnp.float32),
            pltpu.VMEM((n_lt, CONV_HALO, LANES), jnp.float32),
            pltpu.VMEM((n_lt, TIME_BLOCK, LANES), jnp.float32),
        ],
        compiler_params=_params("arbitrary"),
        name="dwconv_ln_silu",
    )(v, dw, _lane_tiles(dw_b), _lane_tiles(ln_g), _lane_tiles(ln_b), *c_args)
    return outs if side_casts else outs[0]


def _conv_out_kernel(v_ref, w_ref, b_ref, x_ref, o_ref):
    y = jnp.dot(v_ref[...], w_ref[...], preferred_element_type=jnp.float32)
    o_ref[...] = x_ref[...] + (y + b_ref[...])


def _conv_out(v, w_out, b_out, x):
    return pl.pallas_call(
        _conv_out_kernel,
        grid=(SEQ // TM, D_MODEL // TN),
        in_specs=[
            pl.BlockSpec((TM, D_MODEL), lambda i, n: (i, 0)),
            pl.BlockSpec((D_MODEL, TN), lambda i, n: (0, n)),
            pl.BlockSpec((1, TN), lambda i, n: (0, n)),
            pl.BlockSpec((TM, TN), lambda i, n: (i, n)),
        ],
        out_specs=pl.BlockSpec((TM, TN), lambda i, n: (i, n)),
        out_shape=jax.ShapeDtypeStruct((SEQ, D_MODEL), jnp.float32),
        compiler_params=_params("parallel", "parallel"),
        name="conv_out",
    )(v, w_out, b_out, x)


def _ssm_stages(h_in_ref, skip_refs, bw_ref, cw_ref, ap_ref, o_ref, carry_ref, keep_carry,
                bu_w_ref, bu_r_ref, s_w_ref, s_r_ref):
    ns = SSM_TILE_STATES
    x_ref, rstd_ref, dg_ref = skip_refs

    def project_in(n):
        cols = slice(n * SSM_IN_COLS, (n + 1) * SSM_IN_COLS)
        bu_w_ref[:, cols] = jnp.dot(h_in_ref[...], bw_ref[0, :, cols], preferred_element_type=jnp.float32)

    a_re = ap_ref[0, 0]
    a_im = ap_ref[0, 1]

    def advance(i, s_re, s_im):
        rows = slice(i * SUBLANES, (i + 1) * SUBLANES)
        b_re = bu_r_ref[rows, 0:ns]
        b_im = bu_r_ref[rows, ns:2 * ns]
        return (a_re * s_re - a_im * s_im + b_re, a_re * s_im + a_im * s_re + b_im)

    zero = jnp.zeros((SUBLANES, ns), jnp.float32)
    in_re = carry_ref[0]
    in_im = carry_ref[1]
    f_re, f_im = in_re, in_im
    n_in = 2 * ns // SSM_IN_COLS
    for i in range(TIME_SEG):
        f_re, f_im = advance(i, f_re, f_im)
        if i % (TIME_SEG // n_in) == 0:
            project_in(i // (TIME_SEG // n_in))

    for level, shift in enumerate((1, 2, 4)):
        p_re = ap_ref[0, 2 + 2 * level]
        p_im = ap_ref[0, 3 + 2 * level]
        r_re = pltpu.roll(f_re, shift, 0)
        r_im = pltpu.roll(f_im, shift, 0)
        f_re, f_im = (f_re + (p_re * r_re - p_im * r_im), f_im + (p_re * r_im + p_im * r_re))

    row = lax.broadcasted_iota(jnp.int32, (SUBLANES, ns), 0)
    e_re = pltpu.roll(f_re, 1, 0)
    e_im = pltpu.roll(f_im, 1, 0)
    hand_over = (row == 0) & keep_carry
    carry_ref[0] = jnp.where(hand_over, e_re, zero)
    carry_ref[1] = jnp.where(hand_over, e_im, zero)
    s_re = jnp.where(row == 0, in_re, e_re)
    s_im = jnp.where(row == 0, in_im, e_im)

    o_ref[...] = jnp.dot(s_r_ref[...], cw_ref[0], preferred_element_type=jnp.float32)

    for i2 in range(TIME_SEG // 2):
        m_re, m_im = advance(2 * i2, s_re, s_im)
        s_re, s_im = advance(2 * i2 + 1, m_re, m_im)
        rows = slice(i2 * 2 * SUBLANES, (i2 + 1) * 2 * SUBLANES)
        s_w_ref[rows, 0:ns] = jnp.concatenate([m_re, s_re], axis=0).astype(jnp.bfloat16)
        s_w_ref[rows, ns:2 * ns] = jnp.concatenate([m_im, s_im], axis=0).astype(jnp.bfloat16)

    rstd = jnp.concatenate([rstd_ref[...]] * (SSM_TILE_CH // LANES), axis=1)
    o_ref[...] += dg_ref[...] * (x_ref[...] * rstd)


def _ssm_kernel(h_in_ref, x_ref, bw_ref, cw_ref, ap_ref, dg_ref, rstd_ref, o_ref,
                bu0_ref, bu1_ref, s0_ref, s1_ref, carry_ref):
    g = pl.program_id(0)

    @pl.when(g == 0)
    def _():
        for ref in (bu0_ref, bu1_ref, s0_ref, s1_ref, carry_ref):
            ref[...] = jnp.zeros(ref.shape, ref.dtype)

    keep_carry = g % (SEQ // TIME_BLOCK) != 0
    stages = functools.partial(_ssm_stages, h_in_ref, (x_ref, rstd_ref, dg_ref), bw_ref, cw_ref,
                               ap_ref, o_ref, carry_ref, keep_carry)

    @pl.when(g % 2 == 0)
    def _():
        stages(bu0_ref, bu1_ref, s1_ref, s0_ref)

    @pl.when(g % 2 == 1)
    def _():
        stages(bu1_ref, bu0_ref, s0_ref, s1_ref)


def _ssm_core(h_bf16, x, rstd, d_gain, bw, cw, ap):
    ns = SSM_TILE_STATES
    n_time = SEQ // TIME_BLOCK
    n_blocks = SSM_TILES * n_time

    def block(g, lag):
        b = jnp.clip(g - lag, 0, n_blocks - 1)
        return b % n_time, b // n_time

    def h_map(lag):
        return lambda g: block(g, lag)

    def tile_map(lag, rank):
        return lambda g: (block(g, lag)[1],) + (0,) * (rank - 1)

    return pl.pallas_call(
        _ssm_kernel,
        grid=(n_blocks + 2,),
        in_specs=[
            pl.BlockSpec((TIME_BLOCK, SSM_TILE_CH), h_map(0)),
            pl.BlockSpec((TIME_BLOCK, SSM_TILE_CH), h_map(2)),
            pl.BlockSpec((1, SSM_TILE_CH, 2 * ns), tile_map(0, 3)),
            pl.BlockSpec((1, 2 * ns, SSM_TILE_CH), tile_map(2, 3)),
            pl.BlockSpec((1, 8, SUBLANES, ns), tile_map(1, 4)),
            pl.BlockSpec((1, SSM_TILE_CH), lambda g: (0, block(g, 2)[1])),
            pl.BlockSpec((TIME_BLOCK, LANES), lambda g: (block(g, 2)[0], 0)),
        ],
        out_specs=pl.BlockSpec((TIME_BLOCK, SSM_TILE_CH), h_map(2)),
        out_shape=jax.ShapeDtypeStruct((SEQ, D_MODEL), jnp.float32),
        scratch_shapes=[
            pltpu.VMEM((TIME_BLOCK, 2 * ns), jnp.float32),
            pltpu.VMEM((TIME_BLOCK, 2 * ns), jnp.float32),
            pltpu.VMEM((TIME_BLOCK, 2 * ns), jnp.bfloat16),
            pltpu.VMEM((TIME_BLOCK, 2 * ns), jnp.bfloat16),
            pltpu.VMEM((2, SUBLANES, ns), jnp.float32),
        ],
        compiler_params=_params("arbitrary"),
        name="ssm_scan",
    )(h_bf16, x, bw, cw, ap, d_gain, rstd)


def _ssm_tables(lam_re, lam_im, log_dt, b_re, b_im, c_re, c_im):
    f32 = jnp.float32
    dt = jnp.exp(log_dt.astype(f32))[:, None]
    lr = lam_re.astype(f32)
    li = lam_im.astype(f32)
    mag = jnp.exp(lr * dt)
    ab_re = mag * jnp.cos(li * dt)
    ab_im = mag * jnp.sin(li * dt)
    nr = ab_re - 1.0
    ni = ab_im
    den = lr * lr + li * li
    k_re = ((nr * lr + ni * li) / den)[..., None]
    k_im = ((ni * lr - nr * li) / den)[..., None]
    br = b_re.astype(f32)
    bi = b_im.astype(f32)
    bb = jnp.stack([k_re * br - k_im * bi, k_re * bi + k_im * br])
    cc = jnp.stack([c_re.astype(f32), -c_im.astype(f32)])

    tg = SSM_TILE_GROUPS
    ns2 = 2 * SSM_TILE_STATES
    bb = bb.reshape(2, SSM_TILES, tg, SSM_STATE, SSM_GROUP).transpose(1, 4, 0, 2, 3)
    cc = cc.reshape(2, SSM_TILES, tg, SSM_GROUP, SSM_STATE).transpose(1, 0, 2, 4, 3)
    bb = jnp.tile(bb.reshape(SSM_TILES, SSM_GROUP, ns2), (1, tg, 1))
    cc = jnp.tile(cc.reshape(SSM_TILES, ns2, SSM_GROUP), (1, 1, tg))
    ch_group = jnp.arange(SSM_TILE_CH) // SSM_GROUP
    col_group = (jnp.arange(ns2) % SSM_TILE_STATES) // SSM_STATE
    bw = jnp.where(ch_group[:, None] == col_group[None, :], bb, 0.0)
    cw = jnp.where(col_group[:, None] == ch_group[None, :], cc, 0.0)

    def square(z):
        return (z[0] * z[0] - z[1] * z[1], 2.0 * z[0] * z[1])

    a = (ab_re, ab_im)
    p = a
    n = 1
    while n < TIME_SEG:
        p = square(p)
        n *= 2
    assert n == TIME_SEG

    def per_sublane(z):
        z = z.reshape(SSM_TILES, 1, SSM_TILE_STATES)
        return jnp.broadcast_to(z, (SSM_TILES, SUBLANES, SSM_TILE_STATES))

    sub = jnp.arange(SUBLANES)[None, :, None]
    tables = [per_sublane(z) for z in a]
    for shift in (1, 2, 4):
        tables += [jnp.where(sub >= shift, per_sublane(z), 0.0) for z in p]
        p = square(p)
    ap = jnp.stack(tables, axis=1)
    return bw.astype(jnp.bfloat16), cw.astype(jnp.bfloat16), ap


def _ssm_out_kernel(y_ref, wv_ref, wg_ref, x_ref, o_ref, yb_ref):
    @pl.when(pl.program_id(1) == 0)
    def _():
        yb_ref[...] = jax.nn.gelu(y_ref[...]).astype(jnp.bfloat16)

    y = yb_ref[...]
    val = jnp.dot(y, wv_ref[...], preferred_element_type=jnp.float32)
    gate = jnp.dot(y, wg_ref[...], preferred_element_type=jnp.float32)
    o_ref[...] = x_ref[...] + val * jax.nn.sigmoid(gate)


def _ssm_out(y, w_glu, x):
    n_col = D_MODEL // TN
    return pl.pallas_call(
        _ssm_out_kernel,
        grid=(SEQ // TM, n_col),
        in_specs=[
            pl.BlockSpec((TM, D_MODEL), lambda i, n: (i, 0)),
            pl.BlockSpec((D_MODEL, TN), lambda i, n: (0, n)),
            pl.BlockSpec((D_MODEL, TN), lambda i, n: (0, n + n_col)),
            pl.BlockSpec((TM, TN), lambda i, n: (i, n)),
        ],
        out_specs=pl.BlockSpec((TM, TN), lambda i, n: (i, n)),
        out_shape=jax.ShapeDtypeStruct((SEQ, D_MODEL), jnp.float32),
        scratch_shapes=[pltpu.VMEM((TM, D_MODEL), jnp.bfloat16)],
        compiler_params=_params("parallel", "arbitrary"),
        name="ssm_out",
    )(y, w_glu, w_glu, x)


def _mlp_kernel(norm_out, n_casts, x_ref, g_ref, wu_ref, wd_ref, *rest):
    rest = list(rest)
    out_gain_ref = rest.pop(0) if norm_out else None
    cast_src = [rest.pop(0) for _ in range(n_casts)]
    o_ref = rest.pop(0)
    hb_out_ref, rstd_out_ref = (rest.pop(0), rest.pop(0)) if norm_out == "operand" else (None, None)
    cast_dst = [rest.pop(0) for _ in range(n_casts)]
    (h_ref,) = rest
    f = pl.program_id(1)

    @pl.when(f == 0)
    def _():
        x = x_ref[...]
        h_ref[...] = _rms(x, g_ref[...]).astype(jnp.bfloat16)
        o_ref[...] = x

    a = jnp.maximum(jnp.dot(h_ref[...], wu_ref[...], preferred_element_type=jnp.float32), 0.0)
    a = (a * a).astype(jnp.bfloat16)
    o_ref[...] += jnp.dot(a, wd_ref[...], preferred_element_type=jnp.float32)
    _run_side_casts(cast_src, cast_dst)

    if norm_out:
        @pl.when(f == pl.num_programs(1) - 1)
        def _():
            y = o_ref[...]
            rstd = lax.rsqrt(jnp.mean(y * y, axis=-1, keepdims=True) + EPS)
            normed = y * rstd * out_gain_ref[...]
            if norm_out == "final":
                o_ref[...] = normed
            else:
                hb_out_ref[...] = normed.astype(jnp.bfloat16)
                rstd_out_ref[...] = jnp.broadcast_to(rstd, rstd_out_ref.shape)


def _mlp(x, g, w_up, w_down, norm_out=None, out_gain=None, side_casts=()):
    assert (norm_out is None) == (out_gain is None)
    n_rows, n_ff = SEQ // TM_MLP, D_FF // TF_MLP
    gain_spec = pl.BlockSpec((1, D_MODEL), lambda i, f: (0, 0))
    in_specs = [
        pl.BlockSpec((TM_MLP, D_MODEL), lambda i, f: (i, 0)),
        gain_spec,
        pl.BlockSpec((D_MODEL, TF_MLP), lambda i, f: (0, f)),
        pl.BlockSpec((TF_MLP, D_MODEL), lambda i, f: (f, 0)),
    ]
    args = [x, g, w_up, w_down]
    out_specs = [pl.BlockSpec((TM_MLP, D_MODEL), lambda i, f: (i, 0))]
    out_shape = [jax.ShapeDtypeStruct((SEQ, D_MODEL), jnp.float32)]
    if norm_out:
        in_specs.append(gain_spec)
        args.append(out_gain)
    if norm_out == "operand":
        out_specs += [pl.BlockSpec((TM_MLP, D_MODEL), lambda i, f: (i, 0)),
                      pl.BlockSpec((TM_MLP, LANES), lambda i, f: (i, 0))]
        out_shape += [jax.ShapeDtypeStruct((SEQ, D_MODEL), jnp.bfloat16),
                      jax.ShapeDtypeStruct((SEQ, LANES), jnp.float32)]
    c_in, c_args, c_out, c_shapes = _side_cast_specs(side_casts, n_rows, n_ff)
    in_specs, args, out_specs, out_shape = in_specs + c_in, args + c_args, out_specs + c_out, out_shape + c_shapes
    outs = pl.pallas_call(
        functools.partial(_mlp_kernel, norm_out, len(side_casts)),
        grid=(n_rows, n_ff),
        in_specs=in_specs,
        out_specs=out_specs,
        out_shape=out_shape,
        scratch_shapes=[pltpu.VMEM((TM_MLP, D_MODEL), jnp.bfloat16)],
        compiler_params=_params("parallel", "arbitrary"),
        name="mlp",
    )(*args)
    return outs if len(outs) > 1 else outs[0]


def kernel(x, mix_norm, conv_w_in, conv_b_in, conv_dw, conv_dw_b, conv_ln_g, conv_ln_b, conv_w_out, conv_b_out, ssm_lambda_re, ssm_lambda_im, ssm_log_dt, ssm_b_re, ssm_b_im, ssm_c_re, ssm_c_im, ssm_d, ssm_w_glu, mlp_norm, mlp_w_up, mlp_w_down, final_norm):
    assert x.shape == (1, SEQ, D_MODEL) and x.dtype == jnp.float32
    bf16 = jnp.bfloat16
    n_blocks = SEQ // TIME_BLOCK
    x = x.reshape(n_blocks, SUBLANES, TIME_SEG, D_MODEL).transpose(0, 2, 1, 3).reshape(SEQ, D_MODEL)
    w_mix = [conv_w_in[0].astype(bf16), conv_w_out[0].astype(bf16)]
    w_up = w_down = None
    for i in range(DEPTH):
        j = i // 2
        g = mix_norm[i].reshape(1, D_MODEL)
        if i % 2 == 0:
            w_in, w_out = w_mix
            v = _conv_in(x, g, w_in, conv_b_in[j].reshape(1, 2 * D_MODEL))
            v = _dwconv(v, conv_dw[j], conv_dw_b[j].reshape(1, D_MODEL),
                        conv_ln_g[j].reshape(1, D_MODEL), conv_ln_b[j].reshape(1, D_MODEL),
                        side_casts=[] if w_up is not None else [(mlp_w_up, i), (mlp_w_down, i)])
            if w_up is None:
                v, w_up, w_down = v
            x = _conv_out(v, w_out, conv_b_out[j].reshape(1, D_MODEL), x)
        else:
            (w_glu,) = w_mix
            bw, cw, ap = _ssm_tables(ssm_lambda_re[j], ssm_lambda_im[j], ssm_log_dt[j],
                                     ssm_b_re[j], ssm_b_im[j], ssm_c_re[j], ssm_c_im[j])
            y = _ssm_core(h_bf16, x, rstd, ssm_d[j].reshape(1, D_MODEL) * g, bw, cw, ap)
            x = _ssm_out(y, w_glu, x)
        g = mlp_norm[i].reshape(1, D_MODEL)
        if i == DEPTH - 1:
            x = _mlp(x, g, w_up, w_down, norm_out="final", out_gain=final_norm.reshape(1, D_MODEL))
            continue
        nxt = i + 1
        casts = [(mlp_w_up, nxt), (mlp_w_down, nxt)]
        if nxt % 2 == 0:
            casts += [(conv_w_in, nxt // 2), (conv_w_out, nxt // 2)]
            x, w_up, w_down, *w_mix = _mlp(x, g, w_up, w_down, side_casts=casts)
        else:
            casts += [(ssm_w_glu, nxt // 2)]
            x, h_bf16, rstd, w_up, w_down, *w_mix = _mlp(
                x, g, w_up, w_down, norm_out="operand", out_gain=mix_norm[nxt].reshape(1, D_MODEL),
                side_casts=casts)
    x = x.reshape(n_blocks, TIME_SEG, SUBLANES, D_MODEL).transpose(0, 2, 1, 3)
    return x.reshape(1, SEQ, D_MODEL)
```

```python
import functools

import jax
import jax.numpy as jnp
from jax import lax
from jax.experimental import pallas as pl
from jax.experimental.pallas import tpu as pltpu

D_MODEL = 2048
SEQ = 8192
DEPTH = 4
CONV_WIDTH = 31
SSM_GROUP = 16
SSM_GROUPS = D_MODEL // SSM_GROUP
SSM_STATE = 64
D_FF = 4 * D_MODEL
EPS = 1e-6

SUBLANES = 8
LANES = 128
VMEM_LIMIT_BYTES = 56 * 1024 * 1024

TM = 1024
TN = 512
TM_OUT = 512
TM_MLP = 512
TF_MLP = 1024

TIME_BLOCK = 512
TIME_SEG = TIME_BLOCK // SUBLANES

CONV_HALO = 32 * SUBLANES
CONV_ROWS = 128
LN_ROWS = 16
LN_UNROLL = 16

SSM_TILE_GROUPS = 16
SSM_TILE_CH = SSM_TILE_GROUPS * SSM_GROUP
SSM_TILE_STATES = SSM_TILE_GROUPS * SSM_STATE
SSM_TILES = SSM_GROUPS // SSM_TILE_GROUPS
SSM_IN_COLS = 256


def _params(*sem):
    return pltpu.CompilerParams(dimension_semantics=sem, vmem_limit_bytes=VMEM_LIMIT_BYTES)


def _rms(x, g):
    ms = jnp.mean(x * x, axis=-1, keepdims=True)
    return x * lax.rsqrt(ms + EPS) * g


def _side_cast_specs(side_casts, n_rows, n_cols=None):
    in_specs, args, out_specs, out_shapes = [], [], [], []
    for w, layer in side_casts:
        tile = (w.shape[1] // n_rows, w.shape[2] // (n_cols or 1))
        if n_cols is None:
            src_map, dst_map = (lambda i, layer=layer: (layer, i, 0)), (lambda i: (i, 0))
        else:
            src_map, dst_map = (lambda i, n, layer=layer: (layer, i, n)), (lambda i, n: (i, n))
        in_specs.append(pl.BlockSpec((None,) + tile, src_map))
        args.append(w)
        out_specs.append(pl.BlockSpec(tile, dst_map))
        out_shapes.append(jax.ShapeDtypeStruct(w.shape[1:], jnp.bfloat16))
    return in_specs, args, out_specs, out_shapes


def _run_side_casts(src_refs, dst_refs):
    for src_ref, dst_ref in zip(src_refs, dst_refs):
        dst_ref[...] = src_ref[...].astype(jnp.bfloat16)


def _conv_in_kernel(x_ref, g_ref, wa_ref, wg_ref, ba_ref, bg_ref, o_ref, h_ref):
    @pl.when(pl.program_id(1) == 0)
    def _():
        h_ref[...] = _rms(x_ref[...], g_ref[...]).astype(jnp.bfloat16)

    h = h_ref[...]
    a = jnp.dot(h, wa_ref[...], preferred_element_type=jnp.float32) + ba_ref[...]
    gate = jnp.dot(h, wg_ref[...], preferred_element_type=jnp.float32) + bg_ref[...]
    v = a * jax.nn.sigmoid(gate)
    for j in range(TN // LANES):
        o_ref[j] = v[:, j * LANES:(j + 1) * LANES]


def _conv_in(x, g, w_in, b_in):
    n_col = D_MODEL // TN
    return pl.pallas_call(
        _conv_in_kernel,
        grid=(SEQ // TM, n_col),
        in_specs=[
            pl.BlockSpec((TM, D_MODEL), lambda i, n: (i, 0)),
            pl.BlockSpec((1, D_MODEL), lambda i, n: (0, 0)),
            pl.BlockSpec((D_MODEL, TN), lambda i, n: (0, n)),
            pl.BlockSpec((D_MODEL, TN), lambda i, n: (0, n + n_col)),
            pl.BlockSpec((1, TN), lambda i, n: (0, n)),
            pl.BlockSpec((1, TN), lambda i, n: (0, n + n_col)),
        ],
        out_specs=pl.BlockSpec((TN // LANES, TM, LANES), lambda i, n: (n, i, 0)),
        out_shape=jax.ShapeDtypeStruct((D_MODEL // LANES, SEQ, LANES), jnp.float32),
        scratch_shapes=[pltpu.VMEM((TM, D_MODEL), jnp.bfloat16)],
        compiler_params=_params("parallel", "arbitrary"),
        name="conv_in",
    )(x, g, w_in, w_in, b_in, b_in)


def _dwconv_kernel(n_casts, v_ref, dw_ref, dwb_ref, lg_ref, lb_ref, *rest):
    cast_src, (o_ref, *cast_dst, ext_ref, tail_ref, cv_ref) = rest[:n_casts], rest[n_casts:]
    @pl.when(pl.program_id(0) == 0)
    def _():
        tail_ref[...] = jnp.zeros(tail_ref.shape, jnp.float32)

    tail_start = TIME_BLOCK - CONV_HALO
    first_tap = CONV_HALO - (CONV_WIDTH - 1) * SUBLANES
    first_segment = lax.broadcasted_iota(jnp.int32, (SUBLANES, LANES), 0) == 0

    def conv_lane_tile(c, carry):
        ext_ref[c, CONV_HALO:, :] = v_ref[c]
        for m in range(CONV_HALO // SUBLANES):
            rows = slice(m * SUBLANES, (m + 1) * SUBLANES)
            cur = pltpu.roll(v_ref[c, tail_start + m * SUBLANES:tail_start + (m + 1) * SUBLANES, :], 1, 0)
            prev = pltpu.roll(tail_ref[c, rows, :], 1, 0)
            ext_ref[c, rows, :] = jnp.where(first_segment, prev, cur)
        tail_ref[c] = v_ref[c, tail_start:, :]
        for r in range(TIME_BLOCK // CONV_ROWS):
            acc = jnp.zeros((CONV_ROWS, LANES), jnp.float32)
            for k in range(CONV_WIDTH):
                start = r * CONV_ROWS + first_tap + k * SUBLANES
                w = dw_ref[c, k * SUBLANES:(k + 1) * SUBLANES, :]
                acc = acc + ext_ref[c, start:start + CONV_ROWS, :] * jnp.concatenate(
                    [w] * (CONV_ROWS // SUBLANES), axis=0)
            cv_ref[c, r * CONV_ROWS:(r + 1) * CONV_ROWS, :] = acc + dwb_ref[c]
        return carry

    lax.fori_loop(0, D_MODEL // LANES, conv_lane_tile, 0)

    def norm_rows(r, carry):
        r0 = pl.multiple_of(r * LN_ROWS, LN_ROWS)
        c = cv_ref[:, pl.ds(r0, LN_ROWS), :]
        mu = jnp.sum(jnp.sum(c, axis=0), axis=-1, keepdims=True) * (1.0 / D_MODEL)
        xc = c - mu
        var = jnp.sum(jnp.sum(xc * xc, axis=0), axis=-1, keepdims=True) * (1.0 / D_MODEL)
        y = xc * lax.rsqrt(var + EPS) * lg_ref[...] + lb_ref[...]
        y = (y * jax.nn.sigmoid(y)).astype(jnp.bfloat16)
        for j in range(D_MODEL // LANES):
            o_ref[pl.ds(r0, LN_ROWS), j * LANES:(j + 1) * LANES] = y[j]
        return carry

    lax.fori_loop(0, TIME_BLOCK // LN_ROWS, norm_rows, 0, unroll=LN_UNROLL)
    _run_side_casts(cast_src, cast_dst)


def _lane_tiles(p):
    p = p.reshape(-1, D_MODEL // LANES, LANES)
    return p.transpose(1, 0, 2)


def _dwconv(v, dw, dw_b, ln_g, ln_b, side_casts=()):
    n_lt = D_MODEL // LANES
    whole = lambda i: (0, 0, 0)
    dw = _lane_tiles(jnp.repeat(dw, SUBLANES, axis=0))
    c_in, c_args, c_out, c_shapes = _side_cast_specs(side_casts, SEQ // TIME_BLOCK)
    outs = pl.pallas_call(
        functools.partial(_dwconv_kernel, len(side_casts)),
        grid=(SEQ // TIME_BLOCK,),
        in_specs=[
            pl.BlockSpec((n_lt, TIME_BLOCK, LANES), lambda i: (0, i, 0)),
            pl.BlockSpec((n_lt, CONV_WIDTH * SUBLANES, LANES), whole),
            pl.BlockSpec((n_lt, 1, LANES), whole),
            pl.BlockSpec((n_lt, 1, LANES), whole),
            pl.BlockSpec((n_lt, 1, LANES), whole),
        ] + c_in,
        out_specs=[pl.BlockSpec((TIME_BLOCK, D_MODEL), lambda i: (i, 0))] + c_out,
        out_shape=[jax.ShapeDtypeStruct((SEQ, D_MODEL), jnp.bfloat16)] + c_shapes,
        scratch_shapes=[
            pltpu.VMEM((n_lt, CONV_HALO + TIME_BLOCK, LANES), jnp.float32),
            pltpu.VMEM((n_lt, CONV_HALO, LANES), jnp.float32),
            pltpu.VMEM((n_lt, TIME_BLOCK, LANES), jnp.float32),
        ],
        compiler_params=_params("arbitrary"),
        name="dwconv_ln_silu",
    )(v, dw, _lane_tiles(dw_b), _lane_tiles(ln_g), _lane_tiles(ln_b), *c_args)
    return outs if side_casts else outs[0]


def _conv_out_kernel(v_ref, w_ref, b_ref, x_ref, o_ref):
    y = jnp.dot(v_ref[...], w_ref[...], preferred_element_type=jnp.float32)
    o_ref[...] = x_ref[...] + (y + b_ref[...])


def _conv_out(v, w_out, b_out, x):
    rows = pl.BlockSpec((TM_OUT, D_MODEL), lambda i: (i, 0))
    return pl.pallas_call(
        _conv_out_kernel,
        grid=(SEQ // TM_OUT,),
        in_specs=[
            rows,
            pl.BlockSpec((D_MODEL, D_MODEL), lambda i: (0, 0)),
            pl.BlockSpec((1, D_MODEL), lambda i: (0, 0)),
            rows,
        ],
        out_specs=rows,
        out_shape=jax.ShapeDtypeStruct((SEQ, D_MODEL), jnp.float32),
        compiler_params=_params("parallel"),
        name="conv_out",
    )(v, w_out, b_out, x)


def _ssm_stages(h_in_ref, skip_refs, bw_ref, cw_ref, ap_ref, o_ref, carry_ref, keep_carry,
                bu_w_ref, bu_r_ref, s_w_ref, s_r_ref):
    ns = SSM_TILE_STATES
    x_ref, rstd_ref, dg_ref = skip_refs

    def project_in(n):
        cols = slice(n * SSM_IN_COLS, (n + 1) * SSM_IN_COLS)
        bu_w_ref[:, cols] = jnp.dot(h_in_ref[...], bw_ref[0, :, cols], preferred_element_type=jnp.float32)

    a_re = ap_ref[0, 0]
    a_im = ap_ref[0, 1]

    def advance(i, s_re, s_im):
        rows = slice(i * SUBLANES, (i + 1) * SUBLANES)
        b_re = bu_r_ref[rows, 0:ns]
        b_im = bu_r_ref[rows, ns:2 * ns]
        return (a_re * s_re - a_im * s_im + b_re, a_re * s_im + a_im * s_re + b_im)

    zero = jnp.zeros((SUBLANES, ns), jnp.float32)
    in_re = carry_ref[0]
    in_im = carry_ref[1]
    f_re, f_im = in_re, in_im
    n_in = 2 * ns // SSM_IN_COLS
    for i in range(TIME_SEG):
        f_re, f_im = advance(i, f_re, f_im)
        if i % (TIME_SEG // n_in) == 0:
            project_in(i // (TIME_SEG // n_in))

    for level, shift in enumerate((1, 2, 4)):
        p_re = ap_ref[0, 2 + 2 * level]
        p_im = ap_ref[0, 3 + 2 * level]
        r_re = pltpu.roll(f_re, shift, 0)
        r_im = pltpu.roll(f_im, shift, 0)
        f_re, f_im = (f_re + (p_re * r_re - p_im * r_im), f_im + (p_re * r_im + p_im * r_re))

    row = lax.broadcasted_iota(jnp.int32, (SUBLANES, ns), 0)
    e_re = pltpu.roll(f_re, 1, 0)
    e_im = pltpu.roll(f_im, 1, 0)
    hand_over = (row == 0) & keep_carry
    carry_ref[0] = jnp.where(hand_over, e_re, zero)
    carry_ref[1] = jnp.where(hand_over, e_im, zero)
    s_re = jnp.where(row == 0, in_re, e_re)
    s_im = jnp.where(row == 0, in_im, e_im)

    o_ref[...] = jnp.dot(s_r_ref[...], cw_ref[0], preferred_element_type=jnp.float32)

    for i2 in range(TIME_SEG // 2):
        m_re, m_im = advance(2 * i2, s_re, s_im)
        s_re, s_im = advance(2 * i2 + 1, m_re, m_im)
        rows = slice(i2 * 2 * SUBLANES, (i2 + 1) * 2 * SUBLANES)
        s_w_ref[rows, 0:ns] = jnp.concatenate([m_re, s_re], axis=0).astype(jnp.bfloat16)
        s_w_ref[rows, ns:2 * ns] = jnp.concatenate([m_im, s_im], axis=0).astype(jnp.bfloat16)

    rstd = jnp.concatenate([rstd_ref[...]] * (SSM_TILE_CH // LANES), axis=1)
    o_ref[...] += dg_ref[...] * (x_ref[...] * rstd)


def _ssm_kernel(h_in_ref, x_ref, bw_ref, cw_ref, ap_ref, dg_ref, rstd_ref, o_ref,
                bu0_ref, bu1_ref, s0_ref, s1_ref, carry_ref):
    g = pl.program_id(0)

    @pl.when(g == 0)
    def _():
        for ref in (bu0_ref, bu1_ref, s0_ref, s1_ref, carry_ref):
            ref[...] = jnp.zeros(ref.shape, ref.dtype)

    keep_carry = g % (SEQ // TIME_BLOCK) != 0
    stages = functools.partial(_ssm_stages, h_in_ref, (x_ref, rstd_ref, dg_ref), bw_ref, cw_ref,
                               ap_ref, o_ref, carry_ref, keep_carry)

    @pl.when(g % 2 == 0)
    def _():
        stages(bu0_ref, bu1_ref, s1_ref, s0_ref)

    @pl.when(g % 2 == 1)
    def _():
        stages(bu1_ref, bu0_ref, s0_ref, s1_ref)


def _ssm_core(h_bf16, x, rstd, d_gain, bw, cw, ap):
    ns = SSM_TILE_STATES
    n_time = SEQ // TIME_BLOCK
    n_blocks = SSM_TILES * n_time

    def block(g, lag):
        b = jnp.clip(g - lag, 0, n_blocks - 1)
        return b % n_time, b // n_time

    def h_map(lag):
        return lambda g: block(g, lag)

    def tile_map(lag, rank):
        return lambda g: (block(g, lag)[1],) + (0,) * (rank - 1)

    return pl.pallas_call(
        _ssm_kernel,
        grid=(n_blocks + 2,),
        in_specs=[
            pl.BlockSpec((TIME_BLOCK, SSM_TILE_CH), h_map(0)),
            pl.BlockSpec((TIME_BLOCK, SSM_TILE_CH), h_map(2)),
            pl.BlockSpec((1, SSM_TILE_CH, 2 * ns), tile_map(0, 3)),
            pl.BlockSpec((1, 2 * ns, SSM_TILE_CH), tile_map(2, 3)),
            pl.BlockSpec((1, 8, SUBLANES, ns), tile_map(1, 4)),
            pl.BlockSpec((1, SSM_TILE_CH), lambda g: (0, block(g, 2)[1])),
            pl.BlockSpec((TIME_BLOCK, LANES), lambda g: (block(g, 2)[0], 0)),
        ],
        out_specs=pl.BlockSpec((TIME_BLOCK, SSM_TILE_CH), h_map(2)),
        out_shape=jax.ShapeDtypeStruct((SEQ, D_MODEL), jnp.float32),
        scratch_shapes=[
            pltpu.VMEM((TIME_BLOCK, 2 * ns), jnp.float32),
            pltpu.VMEM((TIME_BLOCK, 2 * ns), jnp.float32),
            pltpu.VMEM((TIME_BLOCK, 2 * ns), jnp.bfloat16),
            pltpu.VMEM((TIME_BLOCK, 2 * ns), jnp.bfloat16),
            pltpu.VMEM((2, SUBLANES, ns), jnp.float32),
        ],
        compiler_params=_params("arbitrary"),
        name="ssm_scan",
    )(h_bf16, x, bw, cw, ap, d_gain, rstd)


def _ssm_tables(lam_re, lam_im, log_dt, b_re, b_im, c_re, c_im):
    f32 = jnp.float32
    dt = jnp.exp(log_dt.astype(f32))[:, None]
    lr = lam_re.astype(f32)
    li = lam_im.astype(f32)
    mag = jnp.exp(lr * dt)
    ab_re = mag * jnp.cos(li * dt)
    ab_im = mag * jnp.sin(li * dt)
    nr = ab_re - 1.0
    ni = ab_im
    den = lr * lr + li * li
    k_re = ((nr * lr + ni * li) / den)[..., None]
    k_im = ((ni * lr - nr * li) / den)[..., None]
    br = b_re.astype(f32)
    bi = b_im.astype(f32)
    bb = jnp.stack([k_re * br - k_im * bi, k_re * bi + k_im * br])
    cc = jnp.stack([c_re.astype(f32), -c_im.astype(f32)])

    tg = SSM_TILE_GROUPS
    ns2 = 2 * SSM_TILE_STATES
    bb = bb.reshape(2, SSM_TILES, tg, SSM_STATE, SSM_GROUP).transpose(1, 4, 0, 2, 3)
    cc = cc.reshape(2, SSM_TILES, tg, SSM_GROUP, SSM_STATE).transpose(1, 0, 2, 4, 3)
    bb = jnp.tile(bb.reshape(SSM_TILES, SSM_GROUP, ns2), (1, tg, 1))
    cc = jnp.tile(cc.reshape(SSM_TILES, ns2, SSM_GROUP), (1, 1, tg))
    ch_group = jnp.arange(SSM_TILE_CH) // SSM_GROUP
    col_group = (jnp.arange(ns2) % SSM_TILE_STATES) // SSM_STATE
    bw = jnp.where(ch_group[:, None] == col_group[None, :], bb, 0.0)
    cw = jnp.where(col_group[:, None] == ch_group[None, :], cc, 0.0)

    def square(z):
        return (z[0] * z[0] - z[1] * z[1], 2.0 * z[0] * z[1])

    a = (ab_re, ab_im)
    p = a
    n = 1
    while n < TIME_SEG:
        p = square(p)
        n *= 2
    assert n == TIME_SEG

    def per_sublane(z):
        z = z.reshape(SSM_TILES, 1, SSM_TILE_STATES)
        return jnp.broadcast_to(z, (SSM_TILES, SUBLANES, SSM_TILE_STATES))

    sub = jnp.arange(SUBLANES)[None, :, None]
    tables = [per_sublane(z) for z in a]
    for shift in (1, 2, 4):
        tables += [jnp.where(sub >= shift, per_sublane(z), 0.0) for z in p]
        p = square(p)
    ap = jnp.stack(tables, axis=1)
    return bw.astype(jnp.bfloat16), cw.astype(jnp.bfloat16), ap


def _ssm_out_kernel(y_ref, wv_ref, wg_ref, x_ref, o_ref, yb_ref):
    @pl.when(pl.program_id(1) == 0)
    def _():
        yb_ref[...] = jax.nn.gelu(y_ref[...]).astype(jnp.bfloat16)

    y = yb_ref[...]
    val = jnp.dot(y, wv_ref[...], preferred_element_type=jnp.float32)
    gate = jnp.dot(y, wg_ref[...], preferred_element_type=jnp.float32)
    o_ref[...] = x_ref[...] + val * jax.nn.sigmoid(gate)


def _ssm_out(y, w_glu, x):
    n_col = D_MODEL // TN
    return pl.pallas_call(
        _ssm_out_kernel,
        grid=(SEQ // TM, n_col),
        in_specs=[
            pl.BlockSpec((TM, D_MODEL), lambda i, n: (i, 0)),
            pl.BlockSpec((D_MODEL, TN), lambda i, n: (0, n)),
            pl.BlockSpec((D_MODEL, TN), lambda i, n: (0, n + n_col)),
            pl.BlockSpec((TM, TN), lambda i, n: (i, n)),
        ],
        out_specs=pl.BlockSpec((TM, TN), lambda i, n: (i, n)),
        out_shape=jax.ShapeDtypeStruct((SEQ, D_MODEL), jnp.float32),
        scratch_shapes=[pltpu.VMEM((TM, D_MODEL), jnp.bfloat16)],
        compiler_params=_params("parallel", "arbitrary"),
        name="ssm_out",
    )(y, w_glu, w_glu, x)


def _mlp_kernel(norm_out, n_casts, x_ref, g_ref, wu_ref, wd_ref, *rest):
    rest = list(rest)
    out_gain_ref = rest.pop(0) if norm_out else None
    cast_src = [rest.pop(0) for _ in range(n_casts)]
    o_ref = rest.pop(0)
    hb_out_ref, rstd_out_ref = (rest.pop(0), rest.pop(0)) if norm_out == "operand" else (None, None)
    cast_dst = [rest.pop(0) for _ in range(n_casts)]
    (h_ref,) = rest
    f = pl.program_id(1)

    @pl.when(f == 0)
    def _():
        x = x_ref[...]
        h_ref[...] = _rms(x, g_ref[...]).astype(jnp.bfloat16)
        o_ref[...] = x

    a = jnp.maximum(jnp.dot(h_ref[...], wu_ref[...], preferred_element_type=jnp.float32), 0.0)
    a = (a * a).astype(jnp.bfloat16)
    o_ref[...] += jnp.dot(a, wd_ref[...], preferred_element_type=jnp.float32)
    _run_side_casts(cast_src, cast_dst)

    if norm_out:
        @pl.when(f == pl.num_programs(1) - 1)
        def _():
            y = o_ref[...]
            rstd = lax.rsqrt(jnp.mean(y * y, axis=-1, keepdims=True) + EPS)
            normed = y * rstd * out_gain_ref[...]
            if norm_out == "final":
                o_ref[...] = normed
            else:
                hb_out_ref[...] = normed.astype(jnp.bfloat16)
                rstd_out_ref[...] = jnp.broadcast_to(rstd, rstd_out_ref.shape)


def _mlp(x, g, w_up, w_down, norm_out=None, out_gain=None, side_casts=()):
    assert (norm_out is None) == (out_gain is None)
    n_rows, n_ff = SEQ // TM_MLP, D_FF // TF_MLP
    gain_spec = pl.BlockSpec((1, D_MODEL), lambda i, f: (0, 0))
    in_specs = [
        pl.BlockSpec((TM_MLP, D_MODEL), lambda i, f: (i, 0)),
        gain_spec,
        pl.BlockSpec((D_MODEL, TF_MLP), lambda i, f: (0, f)),
        pl.BlockSpec((TF_MLP, D_MODEL), lambda i, f: (f, 0)),
    ]
    args = [x, g, w_up, w_down]
    out_specs = [pl.BlockSpec((TM_MLP, D_MODEL), lambda i, f: (i, 0))]
    out_shape = [jax.ShapeDtypeStruct((SEQ, D_MODEL), jnp.float32)]
    if norm_out:
        in_specs.append(gain_spec)
        args.append(out_gain)
    if norm_out == "operand":
        out_specs += [pl.BlockSpec((TM_MLP, D_MODEL), lambda i, f: (i, 0)),
                      pl.BlockSpec((TM_MLP, LANES), lambda i, f: (i, 0))]
        out_shape += [jax.ShapeDtypeStruct((SEQ, D_MODEL), jnp.bfloat16),
                      jax.ShapeDtypeStruct((SEQ, LANES), jnp.float32)]
    c_in, c_args, c_out, c_shapes = _side_cast_specs(side_casts, n_rows, n_ff)
    in_specs, args, out_specs, out_shape = in_specs + c_in, args + c_args, out_specs + c_out, out_shape + c_shapes
    outs = pl.pallas_call(
        functools.partial(_mlp_kernel, norm_out, len(side_casts)),
        grid=(n_rows, n_ff),
        in_specs=in_specs,
        out_specs=out_specs,
        out_shape=out_shape,
        scratch_shapes=[pltpu.VMEM((TM_MLP, D_MODEL), jnp.bfloat16)],
        compiler_params=_params("parallel", "arbitrary"),
        name="mlp",
    )(*args)
    return outs if len(outs) > 1 else outs[0]


def kernel(x, mix_norm, conv_w_in, conv_b_in, conv_dw, conv_dw_b, conv_ln_g, conv_ln_b, conv_w_out, conv_b_out, ssm_lambda_re, ssm_lambda_im, ssm_log_dt, ssm_b_re, ssm_b_im, ssm_c_re, ssm_c_im, ssm_d, ssm_w_glu, mlp_norm, mlp_w_up, mlp_w_down, final_norm):
    assert x.shape == (1, SEQ, D_MODEL) and x.dtype == jnp.float32
    bf16 = jnp.bfloat16
    n_blocks = SEQ // TIME_BLOCK
    x = x.reshape(n_blocks, SUBLANES, TIME_SEG, D_MODEL).transpose(0, 2, 1, 3).reshape(SEQ, D_MODEL)
    w_mix = [conv_w_in[0].astype(bf16), conv_w_out[0].astype(bf16)]
    w_up = w_down = None
    for i in range(DEPTH):
        j = i // 2
        g = mix_norm[i].reshape(1, D_MODEL)
        if i % 2 == 0:
            w_in, w_out = w_mix
            v = _conv_in(x, g, w_in, conv_b_in[j].reshape(1, 2 * D_MODEL))
            v = _dwconv(v, conv_dw[j], conv_dw_b[j].reshape(1, D_MODEL),
                        conv_ln_g[j].reshape(1, D_MODEL), conv_ln_b[j].reshape(1, D_MODEL),
                        side_casts=[] if w_up is not None else [(mlp_w_up, i), (mlp_w_down, i)])
            if w_up is None:
                v, w_up, w_down = v
            x = _conv_out(v, w_out, conv_b_out[j].reshape(1, D_MODEL), x)
        else:
            (w_glu,) = w_mix
            bw, cw, ap = _ssm_tables(ssm_lambda_re[j], ssm_lambda_im[j], ssm_log_dt[j],
                                     ssm_b_re[j], ssm_b_im[j], ssm_c_re[j], ssm_c_im[j])
            y = _ssm_core(h_bf16, x, rstd, ssm_d[j].reshape(1, D_MODEL) * g, bw, cw, ap)
            x = _ssm_out(y, w_glu, x)
        g = mlp_norm[i].reshape(1, D_MODEL)
        if i == DEPTH - 1:
            x = _mlp(x, g, w_up, w_down, norm_out="final", out_gain=final_norm.reshape(1, D_MODEL))
            continue
        nxt = i + 1
        casts = [(mlp_w_up, nxt), (mlp_w_down, nxt)]
        if nxt % 2 == 0:
            casts += [(conv_w_in, nxt // 2), (conv_w_out, nxt // 2)]
            x, w_up, w_down, *w_mix = _mlp(x, g, w_up, w_down, side_casts=casts)
        else:
            casts += [(ssm_w_glu, nxt // 2)]
            x, h_bf16, rstd, w_up, w_down, *w_mix = _mlp(
                x, g, w_up, w_down, norm_out="operand", out_gain=mix_norm[nxt].reshape(1, D_MODEL),
                side_casts=casts)
    x = x.reshape(n_blocks, TIME_SEG, SUBLANES, D_MODEL).transpose(0, 2, 1, 3)
    return x.reshape(1, SEQ, D_MODEL)
```

```python
import functools

import jax
import jax.numpy as jnp
from jax import lax
from jax.experimental import pallas as pl
from jax.experimental.pallas import tpu as pltpu

D_MODEL = 2048
SEQ = 8192
DEPTH = 4
CONV_WIDTH = 31
SSM_GROUP = 16
SSM_GROUPS = D_MODEL // SSM_GROUP
SSM_STATE = 64
D_FF = 4 * D_MODEL
EPS = 1e-6

SUBLANES = 8
LANES = 128
VMEM_LIMIT_BYTES = 56 * 1024 * 1024

TM = 1024
TN = 512
TM_IN = 256
TM_OUT = 512
TM_MLP = 512
TF_MLP = 1024

TIME_BLOCK = 512
TIME_SEG = TIME_BLOCK // SUBLANES

CONV_HALO = 32 * SUBLANES
CONV_ROWS = 128
LN_ROWS = 16
LN_UNROLL = 16

SSM_TILE_GROUPS = 16
SSM_TILE_CH = SSM_TILE_GROUPS * SSM_GROUP
SSM_TILE_STATES = SSM_TILE_GROUPS * SSM_STATE
SSM_TILES = SSM_GROUPS // SSM_TILE_GROUPS
SSM_IN_COLS = 256


def _params(*sem):
    return pltpu.CompilerParams(dimension_semantics=sem, vmem_limit_bytes=VMEM_LIMIT_BYTES)


def _rms(x, g):
    ms = jnp.mean(x * x, axis=-1, keepdims=True)
    return x * lax.rsqrt(ms + EPS) * g


def _side_cast_specs(side_casts, n_rows, n_cols=None):
    in_specs, args, out_specs, out_shapes = [], [], [], []
    for w, layer in side_casts:
        tile = (w.shape[1] // n_rows, w.shape[2] // (n_cols or 1))
        if n_cols is None:
            src_map, dst_map = (lambda i, layer=layer: (layer, i, 0)), (lambda i: (i, 0))
        else:
            src_map, dst_map = (lambda i, n, layer=layer: (layer, i, n)), (lambda i, n: (i, n))
        in_specs.append(pl.BlockSpec((None,) + tile, src_map))
        args.append(w)
        out_specs.append(pl.BlockSpec(tile, dst_map))
        out_shapes.append(jax.ShapeDtypeStruct(w.shape[1:], jnp.bfloat16))
    return in_specs, args, out_specs, out_shapes


def _run_side_casts(src_refs, dst_refs):
    for src_ref, dst_ref in zip(src_refs, dst_refs):
        dst_ref[...] = src_ref[...].astype(jnp.bfloat16)


def _conv_in_kernel(x_ref, g_ref, w_ref, b_ref, o_ref, h_ref):
    h_ref[...] = _rms(x_ref[...], g_ref[...]).astype(jnp.bfloat16)
    for n in range(D_MODEL // TN):
        a_cols = slice(n * TN, (n + 1) * TN)
        g_cols = slice(D_MODEL + n * TN, D_MODEL + (n + 1) * TN)
        h = h_ref[...]
        a = jnp.dot(h, w_ref[:, a_cols], preferred_element_type=jnp.float32) + b_ref[:, a_cols]
        gate = jnp.dot(h, w_ref[:, g_cols], preferred_element_type=jnp.float32) + b_ref[:, g_cols]
        v = a * jax.nn.sigmoid(gate)
        for j in range(TN // LANES):
            o_ref[n * (TN // LANES) + j] = v[:, j * LANES:(j + 1) * LANES]


def _conv_in(x, g, w_in, b_in):
    return pl.pallas_call(
        _conv_in_kernel,
        grid=(SEQ // TM_IN,),
        in_specs=[
            pl.BlockSpec((TM_IN, D_MODEL), lambda i: (i, 0)),
            pl.BlockSpec((1, D_MODEL), lambda i: (0, 0)),
            pl.BlockSpec((D_MODEL, 2 * D_MODEL), lambda i: (0, 0)),
            pl.BlockSpec((1, 2 * D_MODEL), lambda i: (0, 0)),
        ],
        out_specs=pl.BlockSpec((D_MODEL // LANES, TM_IN, LANES), lambda i: (0, i, 0)),
        out_shape=jax.ShapeDtypeStruct((D_MODEL // LANES, SEQ, LANES), jnp.float32),
        scratch_shapes=[pltpu.VMEM((TM_IN, D_MODEL), jnp.bfloat16)],
        compiler_params=_params("parallel"),
        name="conv_in",
    )(x, g, w_in, b_in)


def _dwconv_kernel(n_casts, v_ref, dw_ref, dwb_ref, lg_ref, lb_ref, *rest):
    cast_src, (o_ref, *cast_dst, ext_ref, tail_ref, cv_ref) = rest[:n_casts], rest[n_casts:]
    @pl.when(pl.program_id(0) == 0)
    def _():
        tail_ref[...] = jnp.zeros(tail_ref.shape, jnp.float32)

    tail_start = TIME_BLOCK - CONV_HALO
    first_tap = CONV_HALO - (CONV_WIDTH - 1) * SUBLANES
    first_segment = lax.broadcasted_iota(jnp.int32, (SUBLANES, LANES), 0) == 0

    def conv_lane_tile(c, carry):
        ext_ref[c, CONV_HALO:, :] = v_ref[c]
        for m in range(CONV_HALO // SUBLANES):
            rows = slice(m * SUBLANES, (m + 1) * SUBLANES)
            cur = pltpu.roll(v_ref[c, tail_start + m * SUBLANES:tail_start + (m + 1) * SUBLANES, :], 1, 0)
            prev = pltpu.roll(tail_ref[c, rows, :], 1, 0)
            ext_ref[c, rows, :] = jnp.where(first_segment, prev, cur)
        tail_ref[c] = v_ref[c, tail_start:, :]
        for r in range(TIME_BLOCK // CONV_ROWS):
            acc = jnp.zeros((CONV_ROWS, LANES), jnp.float32)
            for k in range(CONV_WIDTH):
                start = r * CONV_ROWS + first_tap + k * SUBLANES
                w = dw_ref[c, k * SUBLANES:(k + 1) * SUBLANES, :]
                acc = acc + ext_ref[c, start:start + CONV_ROWS, :] * jnp.concatenate(
                    [w] * (CONV_ROWS // SUBLANES), axis=0)
            cv_ref[c, r * CONV_ROWS:(r + 1) * CONV_ROWS, :] = acc + dwb_ref[c]
        return carry

    lax.fori_loop(0, D_MODEL // LANES, conv_lane_tile, 0)

    def norm_rows(r, carry):
        r0 = pl.multiple_of(r * LN_ROWS, LN_ROWS)
        c = cv_ref[:, pl.ds(r0, LN_ROWS), :]
        mu = jnp.sum(jnp.sum(c, axis=0), axis=-1, keepdims=True) * (1.0 / D_MODEL)
        xc = c - mu
        var = jnp.sum(jnp.sum(xc * xc, axis=0), axis=-1, keepdims=True) * (1.0 / D_MODEL)
        y = xc * lax.rsqrt(var + EPS) * lg_ref[...] + lb_ref[...]
        y = (y * jax.nn.sigmoid(y)).astype(jnp.bfloat16)
        for j in range(D_MODEL // LANES):
            o_ref[pl.ds(r0, LN_ROWS), j * LANES:(j + 1) * LANES] = y[j]
        return carry

    lax.fori_loop(0, TIME_BLOCK // LN_ROWS, norm_rows, 0, unroll=LN_UNROLL)
    _run_side_casts(cast_src, cast_dst)


def _lane_tiles(p):
    p = p.reshape(-1, D_MODEL // LANES, LANES)
    return p.transpose(1, 0, 2)


def _dwconv(v, dw, dw_b, ln_g, ln_b, side_casts=()):
    n_lt = D_MODEL // LANES
    whole = lambda i: (0, 0, 0)
    dw = _lane_tiles(jnp.repeat(dw, SUBLANES, axis=0))
    c_in, c_args, c_out, c_shapes = _side_cast_specs(side_casts, SEQ // TIME_BLOCK)
    outs = pl.pallas_call(
        functools.partial(_dwconv_kernel, len(side_casts)),
        grid=(SEQ // TIME_BLOCK,),
        in_specs=[
            pl.BlockSpec((n_lt, TIME_BLOCK, LANES), lambda i: (0, i, 0)),
            pl.BlockSpec((n_lt, CONV_WIDTH * SUBLANES, LANES), whole),
            pl.BlockSpec((n_lt, 1, LANES), whole),
            pl.BlockSpec((n_lt, 1, LANES), whole),
            pl.BlockSpec((n_lt, 1, LANES), whole),
        ] + c_in,
        out_specs=[pl.BlockSpec((TIME_BLOCK, D_MODEL), lambda i: (i, 0))] + c_out,
        out_shape=[jax.ShapeDtypeStruct((SEQ, D_MODEL), jnp.bfloat16)] + c_shapes,
        scratch_shapes=[
            pltpu.VMEM((n_lt, CONV_HALO + TIME_BLOCK, LANES), jnp.float32),
            pltpu.VMEM((n_lt, CONV_HALO, LANES), jnp.float32),
            pltpu.VMEM((n_lt, TIME_BLOCK, LANES), jnp.float32),
        ],
        compiler_params=_params("arbitrary"),
        name="dwconv_ln_silu",
    )(v, dw, _lane_tiles(dw_b), _lane_tiles(ln_g), _lane_tiles(ln_b), *c_args)
    return outs if side_casts else outs[0]


def _conv_out_kernel(v_ref, w_ref, b_ref, x_ref, o_ref):
    y = jnp.dot(v_ref[...], w_ref[...], preferred_element_type=jnp.float32)
    o_ref[...] = x_ref[...] + (y + b_ref[...])


def _conv_out(v, w_out, b_out, x):
    rows = pl.BlockSpec((TM_OUT, D_MODEL), lambda i: (i, 0))
    return pl.pallas_call(
        _conv_out_kernel,
        grid=(SEQ // TM_OUT,),
        in_specs=[
            rows,
            pl.BlockSpec((D_MODEL, D_MODEL), lambda i: (0, 0)),
            pl.BlockSpec((1, D_MODEL), lambda i: (0, 0)),
            rows,
        ],
        out_specs=rows,
        out_shape=jax.ShapeDtypeStruct((SEQ, D_MODEL), jnp.float32),
        compiler_params=_params("parallel"),
        name="conv_out",
    )(v, w_out, b_out, x)


def _ssm_stages(h_in_ref, skip_refs, bw_ref, cw_ref, ap_ref, o_ref, carry_ref, keep_carry,
                bu_w_ref, bu_r_ref, s_w_ref, s_r_ref):
    ns = SSM_TILE_STATES
    x_ref, rstd_ref, dg_ref = skip_refs

    def project_in(n):
        cols = slice(n * SSM_IN_COLS, (n + 1) * SSM_IN_COLS)
        bu_w_ref[:, cols] = jnp.dot(h_in_ref[...], bw_ref[0, :, cols], preferred_element_type=jnp.float32)

    a_re = ap_ref[0, 0]
    a_im = ap_ref[0, 1]

    def advance(i, s_re, s_im):
        rows = slice(i * SUBLANES, (i + 1) * SUBLANES)
        b_re = bu_r_ref[rows, 0:ns]
        b_im = bu_r_ref[rows, ns:2 * ns]
        return (a_re * s_re - a_im * s_im + b_re, a_re * s_im + a_im * s_re + b_im)

    zero = jnp.zeros((SUBLANES, ns), jnp.float32)
    in_re = carry_ref[0]
    in_im = carry_ref[1]
    f_re, f_im = in_re, in_im
    n_in = 2 * ns // SSM_IN_COLS
    for i in range(TIME_SEG):
        f_re, f_im = advance(i, f_re, f_im)
        if i % (TIME_SEG // n_in) == 0:
            project_in(i // (TIME_SEG // n_in))

    for level, shift in enumerate((1, 2, 4)):
        p_re = ap_ref[0, 2 + 2 * level]
        p_im = ap_ref[0, 3 + 2 * level]
        r_re = pltpu.roll(f_re, shift, 0)
        r_im = pltpu.roll(f_im, shift, 0)
        f_re, f_im = (f_re + (p_re * r_re - p_im * r_im), f_im + (p_re * r_im + p_im * r_re))

    row = lax.broadcasted_iota(jnp.int32, (SUBLANES, ns), 0)
    e_re = pltpu.roll(f_re, 1, 0)
    e_im = pltpu.roll(f_im, 1, 0)
    hand_over = (row == 0) & keep_carry
    carry_ref[0] = jnp.where(hand_over, e_re, zero)
    carry_ref[1] = jnp.where(hand_over, e_im, zero)
    s_re = jnp.where(row == 0, in_re, e_re)
    s_im = jnp.where(row == 0, in_im, e_im)

    o_ref[...] = jnp.dot(s_r_ref[...], cw_ref[0], preferred_element_type=jnp.float32)

    for i2 in range(TIME_SEG // 2):
        m_re, m_im = advance(2 * i2, s_re, s_im)
        s_re, s_im = advance(2 * i2 + 1, m_re, m_im)
        rows = slice(i2 * 2 * SUBLANES, (i2 + 1) * 2 * SUBLANES)
        s_w_ref[rows, 0:ns] = jnp.concatenate([m_re, s_re], axis=0).astype(jnp.bfloat16)
        s_w_ref[rows, ns:2 * ns] = jnp.concatenate([m_im, s_im], axis=0).astype(jnp.bfloat16)

    rstd = jnp.concatenate([rstd_ref[...]] * (SSM_TILE_CH // LANES), axis=1)
    o_ref[...] += dg_ref[...] * (x_ref[...] * rstd)


def _ssm_kernel(h_in_ref, x_ref, bw_ref, cw_ref, ap_ref, dg_ref, rstd_ref, o_ref,
                bu0_ref, bu1_ref, s0_ref, s1_ref, carry_ref):
    g = pl.program_id(0)

    @pl.when(g == 0)
    def _():
        for ref in (bu0_ref, bu1_ref, s0_ref, s1_ref, carry_ref):
            ref[...] = jnp.zeros(ref.shape, ref.dtype)

    keep_carry = g % (SEQ // TIME_BLOCK) != 0
    stages = functools.partial(_ssm_stages, h_in_ref, (x_ref, rstd_ref, dg_ref), bw_ref, cw_ref,
                               ap_ref, o_ref, carry_ref, keep_carry)

    @pl.when(g % 2 == 0)
    def _():
        stages(bu0_ref, bu1_ref, s1_ref, s0_ref)

    @pl.when(g % 2 == 1)
    def _():
        stages(bu1_ref, bu0_ref, s0_ref, s1_ref)


def _ssm_core(h_bf16, x, rstd, d_gain, bw, cw, ap):
    ns = SSM_TILE_STATES
    n_time = SEQ // TIME_BLOCK
    n_blocks = SSM_TILES * n_time

    def block(g, lag):
        b = jnp.clip(g - lag, 0, n_blocks - 1)
        return b % n_time, b // n_time

    def h_map(lag):
        return lambda g: block(g, lag)

    def tile_map(lag, rank):
        return lambda g: (block(g, lag)[1],) + (0,) * (rank - 1)

    return pl.pallas_call(
        _ssm_kernel,
        grid=(n_blocks + 2,),
        in_specs=[
            pl.BlockSpec((TIME_BLOCK, SSM_TILE_CH), h_map(0)),
            pl.BlockSpec((TIME_BLOCK, SSM_TILE_CH), h_map(2)),
            pl.BlockSpec((1, SSM_TILE_CH, 2 * ns), tile_map(0, 3)),
            pl.BlockSpec((1, 2 * ns, SSM_TILE_CH), tile_map(2, 3)),
            pl.BlockSpec((1, 8, SUBLANES, ns), tile_map(1, 4)),
            pl.BlockSpec((1, SSM_TILE_CH), lambda g: (0, block(g, 2)[1])),
            pl.BlockSpec((TIME_BLOCK, LANES), lambda g: (block(g, 2)[0], 0)),
        ],
        out_specs=pl.BlockSpec((TIME_BLOCK, SSM_TILE_CH), h_map(2)),
        out_shape=jax.ShapeDtypeStruct((SEQ, D_MODEL), jnp.float32),
        scratch_shapes=[
            pltpu.VMEM((TIME_BLOCK, 2 * ns), jnp.float32),
            pltpu.VMEM((TIME_BLOCK, 2 * ns), jnp.float32),
            pltpu.VMEM((TIME_BLOCK, 2 * ns), jnp.bfloat16),
            pltpu.VMEM((TIME_BLOCK, 2 * ns), jnp.bfloat16),
            pltpu.VMEM((2, SUBLANES, ns), jnp.float32),
        ],
        compiler_params=_params("arbitrary"),
        name="ssm_scan",
    )(h_bf16, x, bw, cw, ap, d_gain, rstd)


def _ssm_tables(lam_re, lam_im, log_dt, b_re, b_im, c_re, c_im):
    f32 = jnp.float32
    dt = jnp.exp(log_dt.astype(f32))[:, None]
    lr = lam_re.astype(f32)
    li = lam_im.astype(f32)
    mag = jnp.exp(lr * dt)
    ab_re = mag * jnp.cos(li * dt)
    ab_im = mag * jnp.sin(li * dt)
    nr = ab_re - 1.0
    ni = ab_im
    den = lr * lr + li * li
    k_re = ((nr * lr + ni * li) / den)[..., None]
    k_im = ((ni * lr - nr * li) / den)[..., None]
    br = b_re.astype(f32)
    bi = b_im.astype(f32)
    bb = jnp.stack([k_re * br - k_im * bi, k_re * bi + k_im * br])
    cc = jnp.stack([c_re.astype(f32), -c_im.astype(f32)])

    tg = SSM_TILE_GROUPS
    ns2 = 2 * SSM_TILE_STATES
    bb = bb.reshape(2, SSM_TILES, tg, SSM_STATE, SSM_GROUP).transpose(1, 4, 0, 2, 3)
    cc = cc.reshape(2, SSM_TILES, tg, SSM_GROUP, SSM_STATE).transpose(1, 0, 2, 4, 3)
    bb = jnp.tile(bb.reshape(SSM_TILES, SSM_GROUP, ns2), (1, tg, 1))
    cc = jnp.tile(cc.reshape(SSM_TILES, ns2, SSM_GROUP), (1, 1, tg))
    ch_group = jnp.arange(SSM_TILE_CH) // SSM_GROUP
    col_group = (jnp.arange(ns2) % SSM_TILE_STATES) // SSM_STATE
    bw = jnp.where(ch_group[:, None] == col_group[None, :], bb, 0.0)
    cw = jnp.where(col_group[:, None] == ch_group[None, :], cc, 0.0)

    def square(z):
        return (z[0] * z[0] - z[1] * z[1], 2.0 * z[0] * z[1])

    a = (ab_re, ab_im)
    p = a
    n = 1
    while n < TIME_SEG:
        p = square(p)
        n *= 2
    assert n == TIME_SEG

    def per_sublane(z):
        z = z.reshape(SSM_TILES, 1, SSM_TILE_STATES)
        return jnp.broadcast_to(z, (SSM_TILES, SUBLANES, SSM_TILE_STATES))

    sub = jnp.arange(SUBLANES)[None, :, None]
    tables = [per_sublane(z) for z in a]
    for shift in (1, 2, 4):
        tables += [jnp.where(sub >= shift, per_sublane(z), 0.0) for z in p]
        p = square(p)
    ap = jnp.stack(tables, axis=1)
    return bw.astype(jnp.bfloat16), cw.astype(jnp.bfloat16), ap


def _ssm_out_kernel(y_ref, w_ref, x_ref, o_ref, yb_ref):
    yb_ref[...] = jax.nn.gelu(y_ref[...]).astype(jnp.bfloat16)
    for n in range(D_MODEL // TN):
        v_cols = slice(n * TN, (n + 1) * TN)
        g_cols = slice(D_MODEL + n * TN, D_MODEL + (n + 1) * TN)
        y = yb_ref[...]
        val = jnp.dot(y, w_ref[:, v_cols], preferred_element_type=jnp.float32)
        gate = jnp.dot(y, w_ref[:, g_cols], preferred_element_type=jnp.float32)
        o_ref[:, v_cols] = x_ref[:, v_cols] + val * jax.nn.sigmoid(gate)


def _ssm_out(y, w_glu, x):
    rows = pl.BlockSpec((TM_IN, D_MODEL), lambda i: (i, 0))
    return pl.pallas_call(
        _ssm_out_kernel,
        grid=(SEQ // TM_IN,),
        in_specs=[rows, pl.BlockSpec((D_MODEL, 2 * D_MODEL), lambda i: (0, 0)), rows],
        out_specs=rows,
        out_shape=jax.ShapeDtypeStruct((SEQ, D_MODEL), jnp.float32),
        scratch_shapes=[pltpu.VMEM((TM_IN, D_MODEL), jnp.bfloat16)],
        compiler_params=_params("parallel"),
        name="ssm_out",
    )(y, w_glu, x)


def _mlp_kernel(norm_out, n_casts, x_ref, g_ref, wu_ref, wd_ref, *rest):
    rest = list(rest)
    out_gain_ref = rest.pop(0) if norm_out else None
    cast_src = [rest.pop(0) for _ in range(n_casts)]
    o_ref = rest.pop(0)
    hb_out_ref, rstd_out_ref = (rest.pop(0), rest.pop(0)) if norm_out == "operand" else (None, None)
    cast_dst = [rest.pop(0) for _ in range(n_casts)]
    (h_ref,) = rest
    f = pl.program_id(1)

    @pl.when(f == 0)
    def _():
        x = x_ref[...]
        h_ref[...] = _rms(x, g_ref[...]).astype(jnp.bfloat16)
        o_ref[...] = x

    a = jnp.maximum(jnp.dot(h_ref[...], wu_ref[...], preferred_element_type=jnp.float32), 0.0)
    a = (a * a).astype(jnp.bfloat16)
    o_ref[...] += jnp.dot(a, wd_ref[...], preferred_element_type=jnp.float32)
    _run_side_casts(cast_src, cast_dst)

    if norm_out:
        @pl.when(f == pl.num_programs(1) - 1)
        def _():
            y = o_ref[...]
            rstd = lax.rsqrt(jnp.mean(y * y, axis=-1, keepdims=True) + EPS)
            normed = y * rstd * out_gain_ref[...]
            if norm_out == "final":
                o_ref[...] = normed
            else:
                hb_out_ref[...] = normed.astype(jnp.bfloat16)
                rstd_out_ref[...] = jnp.broadcast_to(rstd, rstd_out_ref.shape)


def _mlp(x, g, w_up, w_down, norm_out=None, out_gain=None, side_casts=()):
    assert (norm_out is None) == (out_gain is None)
    n_rows, n_ff = SEQ // TM_MLP, D_FF // TF_MLP
    gain_spec = pl.BlockSpec((1, D_MODEL), lambda i, f: (0, 0))
    in_specs = [
        pl.BlockSpec((TM_MLP, D_MODEL), lambda i, f: (i, 0)),
        gain_spec,
        pl.BlockSpec((D_MODEL, TF_MLP), lambda i, f: (0, f)),
        pl.BlockSpec((TF_MLP, D_MODEL), lambda i, f: (f, 0)),
    ]
    args = [x, g, w_up, w_down]
    out_specs = [pl.BlockSpec((TM_MLP, D_MODEL), lambda i, f: (i, 0))]
    out_shape = [jax.ShapeDtypeStruct((SEQ, D_MODEL), jnp.float32)]
    if norm_out:
        in_specs.append(gain_spec)
        args.append(out_gain)
    if norm_out == "operand":
        out_specs += [pl.BlockSpec((TM_MLP, D_MODEL), lambda i, f: (i, 0)),
                      pl.BlockSpec((TM_MLP, LANES), lambda i, f: (i, 0))]
        out_shape += [jax.ShapeDtypeStruct((SEQ, D_MODEL), jnp.bfloat16),
                      jax.ShapeDtypeStruct((SEQ, LANES), jnp.float32)]
    c_in, c_args, c_out, c_shapes = _side_cast_specs(side_casts, n_rows, n_ff)
    in_specs, args, out_specs, out_shape = in_specs + c_in, args + c_args, out_specs + c_out, out_shape + c_shapes
    outs = pl.pallas_call(
        functools.partial(_mlp_kernel, norm_out, len(side_casts)),
        grid=(n_rows, n_ff),
        in_specs=in_specs,
        out_specs=out_specs,
        out_shape=out_shape,
        scratch_shapes=[pltpu.VMEM((TM_MLP, D_MODEL), jnp.bfloat16)],
        compiler_params=_params("parallel", "arbitrary"),
        name="mlp",
    )(*args)
    return outs if len(outs) > 1 else outs[0]


def kernel(x, mix_norm, conv_w_in, conv_b_in, conv_dw, conv_dw_b, conv_ln_g, conv_ln_b, conv_w_out, conv_b_out, ssm_lambda_re, ssm_lambda_im, ssm_log_dt, ssm_b_re, ssm_b_im, ssm_c_re, ssm_c_im, ssm_d, ssm_w_glu, mlp_norm, mlp_w_up, mlp_w_down, final_norm):
    assert x.shape == (1, SEQ, D_MODEL) and x.dtype == jnp.float32
    bf16 = jnp.bfloat16
    n_blocks = SEQ // TIME_BLOCK
    x = x.reshape(n_blocks, SUBLANES, TIME_SEG, D_MODEL).transpose(0, 2, 1, 3).reshape(SEQ, D_MODEL)
    w_mix = [conv_w_in[0].astype(bf16), conv_w_out[0].astype(bf16)]
    w_up = w_down = None
    for i in range(DEPTH):
        j = i // 2
        g = mix_norm[i].reshape(1, D_MODEL)
        if i % 2 == 0:
            w_in, w_out = w_mix
            v = _conv_in(x, g, w_in, conv_b_in[j].reshape(1, 2 * D_MODEL))
            v = _dwconv(v, conv_dw[j], conv_dw_b[j].reshape(1, D_MODEL),
                        conv_ln_g[j].reshape(1, D_MODEL), conv_ln_b[j].reshape(1, D_MODEL),
                        side_casts=[] if w_up is not None else [(mlp_w_up, i), (mlp_w_down, i)])
            if w_up is None:
                v, w_up, w_down = v
            x = _conv_out(v, w_out, conv_b_out[j].reshape(1, D_MODEL), x)
        else:
            (w_glu,) = w_mix
            bw, cw, ap = _ssm_tables(ssm_lambda_re[j], ssm_lambda_im[j], ssm_log_dt[j],
                                     ssm_b_re[j], ssm_b_im[j], ssm_c_re[j], ssm_c_im[j])
            y = _ssm_core(h_bf16, x, rstd, ssm_d[j].reshape(1, D_MODEL) * g, bw, cw, ap)
            x = _ssm_out(y, w_glu, x)
        g = mlp_norm[i].reshape(1, D_MODEL)
        if i == DEPTH - 1:
            x = _mlp(x, g, w_up, w_down, norm_out="final", out_gain=final_norm.reshape(1, D_MODEL))
            continue
        nxt = i + 1
        casts = [(mlp_w_up, nxt), (mlp_w_down, nxt)]
        if nxt % 2 == 0:
            casts += [(conv_w_in, nxt // 2), (conv_w_out, nxt // 2)]
            x, w_up, w_down, *w_mix = _mlp(x, g, w_up, w_down, side_casts=casts)
        else:
            casts += [(ssm_w_glu, nxt // 2)]
            x, h_bf16, rstd, w_up, w_down, *w_mix = _mlp(
                x, g, w_up, w_down, norm_out="operand", out_gain=mix_norm[nxt].reshape(1, D_MODEL),
                side_casts=casts)
    x = x.reshape(n_blocks, TIME_SEG, SUBLANES, D_MODEL).transpose(0, 2, 1, 3)
    return x.reshape(1, SEQ, D_MODEL)
```

```python
import functools

import jax
import jax.numpy as jnp
from jax import lax
from jax.experimental import pallas as pl
from jax.experimental.pallas import tpu as pltpu

D_MODEL = 2048
SEQ = 8192
DEPTH = 4
CONV_WIDTH = 31
SSM_GROUP = 16
SSM_GROUPS = D_MODEL // SSM_GROUP
SSM_STATE = 64
D_FF = 4 * D_MODEL
EPS = 1e-6

SUBLANES = 8
LANES = 128
VMEM_LIMIT_BYTES = 56 * 1024 * 1024

TM = 1024
TN = 512
TM_IN = 512
TM_OUT = 512
TM_MLP = 512
TF_MLP = 1024

TIME_BLOCK = 512
TIME_SEG = TIME_BLOCK // SUBLANES

CONV_HALO = 32 * SUBLANES
CONV_ROWS = 128
LN_ROWS = 16
LN_UNROLL = 16

SSM_TILE_GROUPS = 16
SSM_TILE_CH = SSM_TILE_GROUPS * SSM_GROUP
SSM_TILE_STATES = SSM_TILE_GROUPS * SSM_STATE
SSM_TILES = SSM_GROUPS // SSM_TILE_GROUPS
SSM_IN_COLS = 256


def _params(*sem):
    return pltpu.CompilerParams(dimension_semantics=sem, vmem_limit_bytes=VMEM_LIMIT_BYTES)


def _rms(x, g):
    ms = jnp.mean(x * x, axis=-1, keepdims=True)
    return x * lax.rsqrt(ms + EPS) * g


def _side_cast_specs(side_casts, n_rows, n_cols=None):
    in_specs, args, out_specs, out_shapes = [], [], [], []
    for w, layer in side_casts:
        tile = (w.shape[1] // n_rows, w.shape[2] // (n_cols or 1))
        if n_cols is None:
            src_map, dst_map = (lambda i, layer=layer: (layer, i, 0)), (lambda i: (i, 0))
        else:
            src_map, dst_map = (lambda i, n, layer=layer: (layer, i, n)), (lambda i, n: (i, n))
        in_specs.append(pl.BlockSpec((None,) + tile, src_map))
        args.append(w)
        out_specs.append(pl.BlockSpec(tile, dst_map))
        out_shapes.append(jax.ShapeDtypeStruct(w.shape[1:], jnp.bfloat16))
    return in_specs, args, out_specs, out_shapes


def _run_side_casts(src_refs, dst_refs):
    for src_ref, dst_ref in zip(src_refs, dst_refs):
        dst_ref[...] = src_ref[...].astype(jnp.bfloat16)


def _conv_in_kernel(x_ref, g_ref, w_ref, b_ref, o_ref, h_ref):
    h_ref[...] = _rms(x_ref[...], g_ref[...]).astype(jnp.bfloat16)
    for n in range(D_MODEL // TN):
        a_cols = slice(n * TN, (n + 1) * TN)
        g_cols = slice(D_MODEL + n * TN, D_MODEL + (n + 1) * TN)
        h = h_ref[...]
        a = jnp.dot(h, w_ref[:, a_cols], preferred_element_type=jnp.float32) + b_ref[:, a_cols]
        gate = jnp.dot(h, w_ref[:, g_cols], preferred_element_type=jnp.float32) + b_ref[:, g_cols]
        v = a * jax.nn.sigmoid(gate)
        for j in range(TN // LANES):
            o_ref[n * (TN // LANES) + j] = v[:, j * LANES:(j + 1) * LANES]


def _conv_in(x, g, w_in, b_in):
    return pl.pallas_call(
        _conv_in_kernel,
        grid=(SEQ // TM_IN,),
        in_specs=[
            pl.BlockSpec((TM_IN, D_MODEL), lambda i: (i, 0)),
            pl.BlockSpec((1, D_MODEL), lambda i: (0, 0)),
            pl.BlockSpec((D_MODEL, 2 * D_MODEL), lambda i: (0, 0)),
            pl.BlockSpec((1, 2 * D_MODEL), lambda i: (0, 0)),
        ],
        out_specs=pl.BlockSpec((D_MODEL // LANES, TM_IN, LANES), lambda i: (0, i, 0)),
        out_shape=jax.ShapeDtypeStruct((D_MODEL // LANES, SEQ, LANES), jnp.float32),
        scratch_shapes=[pltpu.VMEM((TM_IN, D_MODEL), jnp.bfloat16)],
        compiler_params=_params("parallel"),
        name="conv_in",
    )(x, g, w_in, b_in)


def _dwconv_kernel(n_casts, v_ref, dw_ref, dwb_ref, lg_ref, lb_ref, *rest):
    cast_src, (o_ref, *cast_dst, ext_ref, tail_ref, cv_ref) = rest[:n_casts], rest[n_casts:]
    @pl.when(pl.program_id(0) == 0)
    def _():
        tail_ref[...] = jnp.zeros(tail_ref.shape, jnp.float32)

    tail_start = TIME_BLOCK - CONV_HALO
    first_tap = CONV_HALO - (CONV_WIDTH - 1) * SUBLANES
    first_segment = lax.broadcasted_iota(jnp.int32, (SUBLANES, LANES), 0) == 0

    def conv_lane_tile(c, carry):
        ext_ref[c, CONV_HALO:, :] = v_ref[c]
        for m in range(CONV_HALO // SUBLANES):
            rows = slice(m * SUBLANES, (m + 1) * SUBLANES)
            cur = pltpu.roll(v_ref[c, tail_start + m * SUBLANES:tail_start + (m + 1) * SUBLANES, :], 1, 0)
            prev = pltpu.roll(tail_ref[c, rows, :], 1, 0)
            ext_ref[c, rows, :] = jnp.where(first_segment, prev, cur)
        tail_ref[c] = v_ref[c, tail_start:, :]
        for r in range(TIME_BLOCK // CONV_ROWS):
            acc = jnp.zeros((CONV_ROWS, LANES), jnp.float32)
            for k in range(CONV_WIDTH):
                start = r * CONV_ROWS + first_tap + k * SUBLANES
                w = dw_ref[c, k * SUBLANES:(k + 1) * SUBLANES, :]
                acc = acc + ext_ref[c, start:start + CONV_ROWS, :] * jnp.concatenate(
                    [w] * (CONV_ROWS // SUBLANES), axis=0)
            cv_ref[c, r * CONV_ROWS:(r + 1) * CONV_ROWS, :] = acc + dwb_ref[c]
        return carry

    lax.fori_loop(0, D_MODEL // LANES, conv_lane_tile, 0)

    def norm_rows(r, carry):
        r0 = pl.multiple_of(r * LN_ROWS, LN_ROWS)
        c = cv_ref[:, pl.ds(r0, LN_ROWS), :]
        mu = jnp.sum(jnp.sum(c, axis=0), axis=-1, keepdims=True) * (1.0 / D_MODEL)
        xc = c - mu
        var = jnp.sum(jnp.sum(xc * xc, axis=0), axis=-1, keepdims=True) * (1.0 / D_MODEL)
        y = xc * lax.rsqrt(var + EPS) * lg_ref[...] + lb_ref[...]
        y = (y * jax.nn.sigmoid(y)).astype(jnp.bfloat16)
        for j in range(D_MODEL // LANES):
            o_ref[pl.ds(r0, LN_ROWS), j * LANES:(j + 1) * LANES] = y[j]
        return carry

    lax.fori_loop(0, TIME_BLOCK // LN_ROWS, norm_rows, 0, unroll=LN_UNROLL)
    _run_side_casts(cast_src, cast_dst)


def _lane_tiles(p):
    p = p.reshape(-1, D_MODEL // LANES, LANES)
    return p.transpose(1, 0, 2)


def _dwconv(v, dw, dw_b, ln_g, ln_b, side_casts=()):
    n_lt = D_MODEL // LANES
    whole = lambda i: (0, 0, 0)
    dw = _lane_tiles(jnp.repeat(dw, SUBLANES, axis=0))
    c_in, c_args, c_out, c_shapes = _side_cast_specs(side_casts, SEQ // TIME_BLOCK)
    outs = pl.pallas_call(
        functools.partial(_dwconv_kernel, len(side_casts)),
        grid=(SEQ // TIME_BLOCK,),
        in_specs=[
            pl.BlockSpec((n_lt, TIME_BLOCK, LANES), lambda i: (0, i, 0)),
            pl.BlockSpec((n_lt, CONV_WIDTH * SUBLANES, LANES), whole),
            pl.BlockSpec((n_lt, 1, LANES), whole),
            pl.BlockSpec((n_lt, 1, LANES), whole),
            pl.BlockSpec((n_lt, 1, LANES), whole),
        ] + c_in,
        out_specs=[pl.BlockSpec((TIME_BLOCK, D_MODEL), lambda i: (i, 0))] + c_out,
        out_shape=[jax.ShapeDtypeStruct((SEQ, D_MODEL), jnp.bfloat16)] + c_shapes,
        scratch_shapes=[
            pltpu.VMEM((n_lt, CONV_HALO + TIME_BLOCK, LANES), jnp.float32),
            pltpu.VMEM((n_lt, CONV_HALO, LANES), jnp.float32),
            pltpu.VMEM((n_lt, TIME_BLOCK, LANES), jnp.float32),
        ],
        compiler_params=_params("arbitrary"),
        name="dwconv_ln_silu",
    )(v, dw, _lane_tiles(dw_b), _lane_tiles(ln_g), _lane_tiles(ln_b), *c_args)
    return outs if side_casts else outs[0]


def _conv_out_kernel(v_ref, w_ref, b_ref, x_ref, o_ref):
    y = jnp.dot(v_ref[...], w_ref[...], preferred_element_type=jnp.float32)
    o_ref[...] = x_ref[...] + (y + b_ref[...])


def _conv_out(v, w_out, b_out, x):
    rows = pl.BlockSpec((TM_OUT, D_MODEL), lambda i: (i, 0))
    return pl.pallas_call(
        _conv_out_kernel,
        grid=(SEQ // TM_OUT,),
        in_specs=[
            rows,
            pl.BlockSpec((D_MODEL, D_MODEL), lambda i: (0, 0)),
            pl.BlockSpec((1, D_MODEL), lambda i: (0, 0)),
            rows,
        ],
        out_specs=rows,
        out_shape=jax.ShapeDtypeStruct((SEQ, D_MODEL), jnp.float32),
        compiler_params=_params("parallel"),
        name="conv_out",
    )(v, w_out, b_out, x)


def _ssm_stages(h_in_ref, skip_refs, bw_ref, cw_ref, ap_ref, o_ref, carry_ref, keep_carry,
                bu_w_ref, bu_r_ref, s_w_ref, s_r_ref):
    ns = SSM_TILE_STATES
    x_ref, rstd_ref, dg_ref = skip_refs

    def project_in(n):
        cols = slice(n * SSM_IN_COLS, (n + 1) * SSM_IN_COLS)
        bu_w_ref[:, cols] = jnp.dot(h_in_ref[...], bw_ref[0, :, cols], preferred_element_type=jnp.float32)

    a_re = ap_ref[0, 0]
    a_im = ap_ref[0, 1]

    def advance(i, s_re, s_im):
        rows = slice(i * SUBLANES, (i + 1) * SUBLANES)
        b_re = bu_r_ref[rows, 0:ns]
        b_im = bu_r_ref[rows, ns:2 * ns]
        return (a_re * s_re - a_im * s_im + b_re, a_re * s_im + a_im * s_re + b_im)

    zero = jnp.zeros((SUBLANES, ns), jnp.float32)
    in_re = carry_ref[0]
    in_im = carry_ref[1]
    f_re, f_im = in_re, in_im
    n_in = 2 * ns // SSM_IN_COLS
    for i in range(TIME_SEG):
        f_re, f_im = advance(i, f_re, f_im)
        if i % (TIME_SEG // n_in) == 0:
            project_in(i // (TIME_SEG // n_in))

    for level, shift in enumerate((1, 2, 4)):
        p_re = ap_ref[0, 2 + 2 * level]
        p_im = ap_ref[0, 3 + 2 * level]
        r_re = pltpu.roll(f_re, shift, 0)
        r_im = pltpu.roll(f_im, shift, 0)
        f_re, f_im = (f_re + (p_re * r_re - p_im * r_im), f_im + (p_re * r_im + p_im * r_re))

    row = lax.broadcasted_iota(jnp.int32, (SUBLANES, ns), 0)
    e_re = pltpu.roll(f_re, 1, 0)
    e_im = pltpu.roll(f_im, 1, 0)
    hand_over = (row == 0) & keep_carry
    carry_ref[0] = jnp.where(hand_over, e_re, zero)
    carry_ref[1] = jnp.where(hand_over, e_im, zero)
    s_re = jnp.where(row == 0, in_re, e_re)
    s_im = jnp.where(row == 0, in_im, e_im)

    o_ref[...] = jnp.dot(s_r_ref[...], cw_ref[0], preferred_element_type=jnp.float32)

    for i2 in range(TIME_SEG // 2):
        m_re, m_im = advance(2 * i2, s_re, s_im)
        s_re, s_im = advance(2 * i2 + 1, m_re, m_im)
        rows = slice(i2 * 2 * SUBLANES, (i2 + 1) * 2 * SUBLANES)
        s_w_ref[rows, 0:ns] = jnp.concatenate([m_re, s_re], axis=0).astype(jnp.bfloat16)
        s_w_ref[rows, ns:2 * ns] = jnp.concatenate([m_im, s_im], axis=0).astype(jnp.bfloat16)

    rstd = jnp.concatenate([rstd_ref[...]] * (SSM_TILE_CH // LANES), axis=1)
    o_ref[...] += dg_ref[...] * (x_ref[...] * rstd)


def _ssm_kernel(h_in_ref, x_ref, bw_ref, cw_ref, ap_ref, dg_ref, rstd_ref, o_ref,
                bu0_ref, bu1_ref, s0_ref, s1_ref, carry_ref):
    g = pl.program_id(0)

    @pl.when(g == 0)
    def _():
        for ref in (bu0_ref, bu1_ref, s0_ref, s1_ref, carry_ref):
            ref[...] = jnp.zeros(ref.shape, ref.dtype)

    keep_carry = g % (SEQ // TIME_BLOCK) != 0
    stages = functools.partial(_ssm_stages, h_in_ref, (x_ref, rstd_ref, dg_ref), bw_ref, cw_ref,
                               ap_ref, o_ref, carry_ref, keep_carry)

    @pl.when(g % 2 == 0)
    def _():
        stages(bu0_ref, bu1_ref, s1_ref, s0_ref)

    @pl.when(g % 2 == 1)
    def _():
        stages(bu1_ref, bu0_ref, s0_ref, s1_ref)


def _ssm_core(h_bf16, x, rstd, d_gain, bw, cw, ap):
    ns = SSM_TILE_STATES
    n_time = SEQ // TIME_BLOCK
    n_blocks = SSM_TILES * n_time

    def block(g, lag):
        b = jnp.clip(g - lag, 0, n_blocks - 1)
        return b % n_time, b // n_time

    def h_map(lag):
        return lambda g: block(g, lag)

    def tile_map(lag, rank):
        return lambda g: (block(g, lag)[1],) + (0,) * (rank - 1)

    return pl.pallas_call(
        _ssm_kernel,
        grid=(n_blocks + 2,),
        in_specs=[
            pl.BlockSpec((TIME_BLOCK, SSM_TILE_CH), h_map(0)),
            pl.BlockSpec((TIME_BLOCK, SSM_TILE_CH), h_map(2)),
            pl.BlockSpec((1, SSM_TILE_CH, 2 * ns), tile_map(0, 3)),
            pl.BlockSpec((1, 2 * ns, SSM_TILE_CH), tile_map(2, 3)),
            pl.BlockSpec((1, 8, SUBLANES, ns), tile_map(1, 4)),
            pl.BlockSpec((1, SSM_TILE_CH), lambda g: (0, block(g, 2)[1])),
            pl.BlockSpec((TIME_BLOCK, LANES), lambda g: (block(g, 2)[0], 0)),
        ],
        out_specs=pl.BlockSpec((TIME_BLOCK, SSM_TILE_CH), h_map(2)),
        out_shape=jax.ShapeDtypeStruct((SEQ, D_MODEL), jnp.float32),
        scratch_shapes=[
            pltpu.VMEM((TIME_BLOCK, 2 * ns), jnp.float32),
            pltpu.VMEM((TIME_BLOCK, 2 * ns), jnp.float32),
            pltpu.VMEM((TIME_BLOCK, 2 * ns), jnp.bfloat16),
            pltpu.VMEM((TIME_BLOCK, 2 * ns), jnp.bfloat16),
            pltpu.VMEM((2, SUBLANES, ns), jnp.float32),
        ],
        compiler_params=_params("arbitrary"),
        name="ssm_scan",
    )(h_bf16, x, bw, cw, ap, d_gain, rstd)


def _ssm_tables(lam_re, lam_im, log_dt, b_re, b_im, c_re, c_im):
    f32 = jnp.float32
    dt = jnp.exp(log_dt.astype(f32))[:, None]
    lr = lam_re.astype(f32)
    li = lam_im.astype(f32)
    mag = jnp.exp(lr * dt)
    ab_re = mag * jnp.cos(li * dt)
    ab_im = mag * jnp.sin(li * dt)
    nr = ab_re - 1.0
    ni = ab_im
    den = lr * lr + li * li
    k_re = ((nr * lr + ni * li) / den)[..., None]
    k_im = ((ni * lr - nr * li) / den)[..., None]
    br = b_re.astype(f32)
    bi = b_im.astype(f32)
    bb = jnp.stack([k_re * br - k_im * bi, k_re * bi + k_im * br])
    cc = jnp.stack([c_re.astype(f32), -c_im.astype(f32)])

    tg = SSM_TILE_GROUPS
    ns2 = 2 * SSM_TILE_STATES
    bb = bb.reshape(2, SSM_TILES, tg, SSM_STATE, SSM_GROUP).transpose(1, 4, 0, 2, 3)
    cc = cc.reshape(2, SSM_TILES, tg, SSM_GROUP, SSM_STATE).transpose(1, 0, 2, 4, 3)
    bb = jnp.tile(bb.reshape(SSM_TILES, SSM_GROUP, ns2), (1, tg, 1))
    cc = jnp.tile(cc.reshape(SSM_TILES, ns2, SSM_GROUP), (1, 1, tg))
    ch_group = jnp.arange(SSM_TILE_CH) // SSM_GROUP
    col_group = (jnp.arange(ns2) % SSM_TILE_STATES) // SSM_STATE
    bw = jnp.where(ch_group[:, None] == col_group[None, :], bb, 0.0)
    cw = jnp.where(col_group[:, None] == ch_group[None, :], cc, 0.0)

    def square(z):
        return (z[0] * z[0] - z[1] * z[1], 2.0 * z[0] * z[1])

    a = (ab_re, ab_im)
    p = a
    n = 1
    while n < TIME_SEG:
        p = square(p)
        n *= 2
    assert n == TIME_SEG

    def per_sublane(z):
        z = z.reshape(SSM_TILES, 1, SSM_TILE_STATES)
        return jnp.broadcast_to(z, (SSM_TILES, SUBLANES, SSM_TILE_STATES))

    sub = jnp.arange(SUBLANES)[None, :, None]
    tables = [per_sublane(z) for z in a]
    for shift in (1, 2, 4):
        tables += [jnp.where(sub >= shift, per_sublane(z), 0.0) for z in p]
        p = square(p)
    ap = jnp.stack(tables, axis=1)
    return bw.astype(jnp.bfloat16), cw.astype(jnp.bfloat16), ap


def _ssm_out_kernel(y_ref, w_ref, x_ref, o_ref, yb_ref):
    yb_ref[...] = jax.nn.gelu(y_ref[...]).astype(jnp.bfloat16)
    for n in range(D_MODEL // TN):
        v_cols = slice(n * TN, (n + 1) * TN)
        g_cols = slice(D_MODEL + n * TN, D_MODEL + (n + 1) * TN)
        y = yb_ref[...]
        val = jnp.dot(y, w_ref[:, v_cols], preferred_element_type=jnp.float32)
        gate = jnp.dot(y, w_ref[:, g_cols], preferred_element_type=jnp.float32)
        o_ref[:, v_cols] = x_ref[:, v_cols] + val * jax.nn.sigmoid(gate)


def _ssm_out(y, w_glu, x):
    rows = pl.BlockSpec((TM_IN, D_MODEL), lambda i: (i, 0))
    return pl.pallas_call(
        _ssm_out_kernel,
        grid=(SEQ // TM_IN,),
        in_specs=[rows, pl.BlockSpec((D_MODEL, 2 * D_MODEL), lambda i: (0, 0)), rows],
        out_specs=rows,
        out_shape=jax.ShapeDtypeStruct((SEQ, D_MODEL), jnp.float32),
        scratch_shapes=[pltpu.VMEM((TM_IN, D_MODEL), jnp.bfloat16)],
        compiler_params=_params("parallel"),
        name="ssm_out",
    )(y, w_glu, x)


def _mlp_kernel(norm_out, n_casts, x_ref, g_ref, wu_ref, wd_ref, *rest):
    rest = list(rest)
    out_gain_ref = rest.pop(0) if norm_out else None
    cast_src = [rest.pop(0) for _ in range(n_casts)]
    o_ref = rest.pop(0)
    hb_out_ref, rstd_out_ref = (rest.pop(0), rest.pop(0)) if norm_out == "operand" else (None, None)
    cast_dst = [rest.pop(0) for _ in range(n_casts)]
    (h_ref,) = rest
    f = pl.program_id(1)

    @pl.when(f == 0)
    def _():
        x = x_ref[...]
        h_ref[...] = _rms(x, g_ref[...]).astype(jnp.bfloat16)
        o_ref[...] = x

    a = jnp.maximum(jnp.dot(h_ref[...], wu_ref[...], preferred_element_type=jnp.float32), 0.0)
    a = (a * a).astype(jnp.bfloat16)
    o_ref[...] += jnp.dot(a, wd_ref[...], preferred_element_type=jnp.float32)
    _run_side_casts(cast_src, cast_dst)

    if norm_out:
        @pl.when(f == pl.num_programs(1) - 1)
        def _():
            y = o_ref[...]
            rstd = lax.rsqrt(jnp.mean(y * y, axis=-1, keepdims=True) + EPS)
            normed = y * rstd * out_gain_ref[...]
            if norm_out == "final":
                o_ref[...] = normed
            else:
                hb_out_ref[...] = normed.astype(jnp.bfloat16)
                rstd_out_ref[...] = jnp.broadcast_to(rstd, rstd_out_ref.shape)


def _mlp(x, g, w_up, w_down, norm_out=None, out_gain=None, side_casts=()):
    assert (norm_out is None) == (out_gain is None)
    n_rows, n_ff = SEQ // TM_MLP, D_FF // TF_MLP
    gain_spec = pl.BlockSpec((1, D_MODEL), lambda i, f: (0, 0))
    in_specs = [
        pl.BlockSpec((TM_MLP, D_MODEL), lambda i, f: (i, 0)),
        gain_spec,
        pl.BlockSpec((D_MODEL, TF_MLP), lambda i, f: (0, f)),
        pl.BlockSpec((TF_MLP, D_MODEL), lambda i, f: (f, 0)),
    ]
    args = [x, g, w_up, w_down]
    out_specs = [pl.BlockSpec((TM_MLP, D_MODEL), lambda i, f: (i, 0))]
    out_shape = [jax.ShapeDtypeStruct((SEQ, D_MODEL), jnp.float32)]
    if norm_out:
        in_specs.append(gain_spec)
        args.append(out_gain)
    if norm_out == "operand":
        out_specs += [pl.BlockSpec((TM_MLP, D_MODEL), lambda i, f: (i, 0)),
                      pl.BlockSpec((TM_MLP, LANES), lambda i, f: (i, 0))]
        out_shape += [jax.ShapeDtypeStruct((SEQ, D_MODEL), jnp.bfloat16),
                      jax.ShapeDtypeStruct((SEQ, LANES), jnp.float32)]
    c_in, c_args, c_out, c_shapes = _side_cast_specs(side_casts, n_rows, n_ff)
    in_specs, args, out_specs, out_shape = in_specs + c_in, args + c_args, out_specs + c_out, out_shape + c_shapes
    outs = pl.pallas_call(
        functools.partial(_mlp_kernel, norm_out, len(side_casts)),
        grid=(n_rows, n_ff),
        in_specs=in_specs,
        out_specs=out_specs,
        out_shape=out_shape,
        scratch_shapes=[pltpu.VMEM((TM_MLP, D_MODEL), jnp.bfloat16)],
        compiler_params=_params("parallel", "arbitrary"),
        name="mlp",
    )(*args)
    return outs if len(outs) > 1 else outs[0]


def kernel(x, mix_norm, conv_w_in, conv_b_in, conv_dw, conv_dw_b, conv_ln_g, conv_ln_b, conv_w_out, conv_b_out, ssm_lambda_re, ssm_lambda_im, ssm_log_dt, ssm_b_re, ssm_b_im, ssm_c_re, ssm_c_im, ssm_d, ssm_w_glu, mlp_norm, mlp_w_up, mlp_w_down, final_norm):
    assert x.shape == (1, SEQ, D_MODEL) and x.dtype == jnp.float32
    bf16 = jnp.bfloat16
    n_blocks = SEQ // TIME_BLOCK
    x = x.reshape(n_blocks, SUBLANES, TIME_SEG, D_MODEL).transpose(0, 2, 1, 3).reshape(SEQ, D_MODEL)
    w_mix = [conv_w_in[0].astype(bf16), conv_w_out[0].astype(bf16)]
    w_up = w_down = None
    for i in range(DEPTH):
        j = i // 2
        g = mix_norm[i].reshape(1, D_MODEL)
        if i % 2 == 0:
            w_in, w_out = w_mix
            v = _conv_in(x, g, w_in, conv_b_in[j].reshape(1, 2 * D_MODEL))
            v = _dwconv(v, conv_dw[j], conv_dw_b[j].reshape(1, D_MODEL),
                        conv_ln_g[j].reshape(1, D_MODEL), conv_ln_b[j].reshape(1, D_MODEL),
                        side_casts=[] if w_up is not None else [(mlp_w_up, i), (mlp_w_down, i)])
            if w_up is None:
                v, w_up, w_down = v
            x = _conv_out(v, w_out, conv_b_out[j].reshape(1, D_MODEL), x)
        else:
            (w_glu,) = w_mix
            bw, cw, ap = _ssm_tables(ssm_lambda_re[j], ssm_lambda_im[j], ssm_log_dt[j],
                                     ssm_b_re[j], ssm_b_im[j], ssm_c_re[j], ssm_c_im[j])
            y = _ssm_core(h_bf16, x, rstd, ssm_d[j].reshape(1, D_MODEL) * g, bw, cw, ap)
            x = _ssm_out(y, w_glu, x)
        g = mlp_norm[i].reshape(1, D_MODEL)
        if i == DEPTH - 1:
            x = _mlp(x, g, w_up, w_down, norm_out="final", out_gain=final_norm.reshape(1, D_MODEL))
            continue
        nxt = i + 1
        casts = [(mlp_w_up, nxt), (mlp_w_down, nxt)]
        if nxt % 2 == 0:
            casts += [(conv_w_in, nxt // 2), (conv_w_out, nxt // 2)]
            x, w_up, w_down, *w_mix = _mlp(x, g, w_up, w_down, side_casts=casts)
        else:
            casts += [(ssm_w_glu, nxt // 2)]
            x, h_bf16, rstd, w_up, w_down, *w_mix = _mlp(
                x, g, w_up, w_down, norm_out="operand", out_gain=mix_norm[nxt].reshape(1, D_MODEL),
                side_casts=casts)
    x = x.reshape(n_blocks, TIME_SEG, SUBLANES, D_MODEL).transpose(0, 2, 1, 3)
    return x.reshape(1, SEQ, D_MODEL)
```
